```python
import math
import jax, jax.numpy as jnp
from jax import lax
import numpy as np

D_MODEL = 1024
BATCH = 4
SEQ = 4096
DEPTH = 4
DEC_BATCH = 128
DEC_SEQ = 8
PAST_LEN = 8192
PAGE_SIZE = 128

H_A = 4
DK_A = D_MODEL // (2 * H_A)
DV_A = D_MODEL // H_A
GLA_QK = H_A * DK_A
GLA_V = H_A * DV_A
GATE_RANK = 16
GATE_TAU = 16.0
GLA_CHUNK = 64

HD_B = 64
H_B = D_MODEL // HD_B
KV_B = 4
G_B = H_B // KV_B
SWA_Q = H_B * HD_B
SWA_KV = KV_B * HD_B
WINDOW = 128
ROT_DIM = HD_B // 4
ROPE_THETA = 500000.0

N_GROUPS = 4
EXPERTS_PER_GROUP = 8
N_EXPERTS = N_GROUPS * EXPERTS_PER_GROUP
TOP_K = 2
EXPERT_FF = D_MODEL // 4
MOE_BLOCK = 128

ALPHA = (2 * DEPTH) ** 0.25
BETA = (8 * DEPTH) ** -0.25
LN_EPS = 1e-5
NORM_EPS = 1e-6

IN_SPLIT = (GLA_QK, GLA_QK, GLA_V, GLA_V, GATE_RANK, SWA_Q, SWA_KV, SWA_KV, D_MODEL, D_MODEL)
N_IN = 2 * GLA_QK + 2 * GLA_V + GATE_RANK + SWA_Q + 2 * SWA_KV + 2 * D_MODEL

kernel_name = 'hybrid_gla_swa_sink_hmoe_step'


def _layer_norm(x, g, b):
    xf = x.astype(jnp.float32)
    mu = xf.mean(-1, keepdims=True)
    var = jnp.square(xf - mu).mean(-1, keepdims=True)
    return ((xf - mu) * lax.rsqrt(var + LN_EPS) * g.astype(jnp.float32) + b.astype(jnp.float32)).astype(x.dtype)


def _rope(x, pos):
    half = ROT_DIM // 2
    inv = ROPE_THETA ** (-jnp.arange(half, dtype=jnp.float32) * 2.0 / ROT_DIM)
    ang = pos.astype(jnp.float32)[:, None] * inv[None, :]
    cos = jnp.cos(ang)[None, :, None, :]
    sin = jnp.sin(ang)[None, :, None, :]
    xr = x[..., :ROT_DIM].astype(jnp.float32)
    x1, x2 = xr[..., :half], xr[..., half:]
    rot = jnp.concatenate([x1 * cos - x2 * sin, x2 * cos + x1 * sin], axis=-1).astype(x.dtype)
    return jnp.concatenate([rot, x[..., ROT_DIM:]], axis=-1)


def _gla(q, k, v, logf, s0):
    B, T, H, _ = q.shape
    c = GLA_CHUNK if T % GLA_CHUNK == 0 else T
    nc = T // c

    def chunks(a):
        return a.astype(jnp.float32).reshape(B, nc, c, H, a.shape[-1]).transpose(1, 0, 3, 2, 4)

    qc, kc, vc, fc = chunks(q), chunks(k), chunks(v), chunks(logf)
    causal = jnp.tril(jnp.ones((c, c), dtype=bool))

    def step(S, inp):
        qi, ki, vi, fi = inp
        b = jnp.cumsum(fi, axis=2)
        inter = jnp.einsum('bhid,bhde->bhie', qi * jnp.exp(b), S)
        diff = jnp.where(causal[:, :, None], b[:, :, :, None, :] - b[:, :, None, :, :], -jnp.inf)
        attn = jnp.einsum('bhid,bhjd,bhijd->bhij', qi, ki, jnp.exp(diff))
        intra = jnp.einsum('bhij,bhje->bhie', attn, vi)
        bl = b[:, :, -1:, :]
        S = jnp.exp(bl[:, :, 0, :])[..., None] * S + jnp.einsum('bhjd,bhje->bhde', ki * jnp.exp(bl - b), vi)
        return S, inter + intra

    S, o = lax.scan(step, s0.astype(jnp.float32), (qc, kc, vc, fc))
    o = o.transpose(1, 0, 3, 2, 4).reshape(B, T, H, -1)
    return o, S


def _sink_attention(q6, kk, vv, valid, sinks):
    s = jnp.einsum('bnqkgd,bnskd->bnkgqs', q6, kk).astype(jnp.float32) * (HD_B ** -0.5)
    s = jnp.where(valid[None, :, None, None], s, -jnp.inf)
    sk = sinks.astype(jnp.float32).reshape(KV_B, G_B)[None, None, :, :, None, None]
    m = jnp.maximum(s.max(-1, keepdims=True), sk)
    p = jnp.exp(s - m)
    den = p.sum(-1, keepdims=True) + jnp.exp(sk - m)
    return jnp.einsum('bnkgqs,bnskd->bnqkgd', (p / den).astype(vv.dtype), vv)


def _mixer_layer(x, pos, s0, k_buf, v_buf, w_in, w_gf2, b_gf, gla_norm_g, sinks, b_merge,
                 w_branch_a, w_branch_b, w_out, ln_g, ln_b):
    B, T, _ = x.shape
    pts = []
    acc = 0
    for s in IN_SPLIT[:-1]:
        acc += s
        pts.append(acc)
    gq, gk, gv, gr, glr, sq, sk, sv, ga, gb = jnp.split(x @ w_in, pts, axis=-1)

    logf = jax.nn.log_sigmoid((glr @ w_gf2 + b_gf).astype(jnp.float32)) / GATE_TAU
    o_a, s_new = _gla(gq.reshape(B, T, H_A, DK_A) * (DK_A ** -0.5), gk.reshape(B, T, H_A, DK_A),
                      gv.reshape(B, T, H_A, DV_A), logf.reshape(B, T, H_A, DK_A), s0)
    o_a = o_a * lax.rsqrt(jnp.mean(jnp.square(o_a), -1, keepdims=True) + NORM_EPS) * gla_norm_g.astype(jnp.float32)
    o_a = o_a.reshape(B, T, GLA_V).astype(x.dtype) * jax.nn.silu(gr)
    y_a = o_a @ w_branch_a

    q = _rope(sq.reshape(B, T, H_B, HD_B), pos)
    k = _rope(sk.reshape(B, T, KV_B, HD_B), pos)
    v = sv.reshape(B, T, KV_B, HD_B)
    if k_buf is None:
        nb = T // WINDOW
        kb = k.reshape(B, nb, WINDOW, KV_B, HD_B)
        vb = v.reshape(B, nb, WINDOW, KV_B, HD_B)
        padw = ((0, 0), (1, 0), (0, 0), (0, 0), (0, 0))
        kk = jnp.concatenate([jnp.pad(kb, padw)[:, :-1], kb], axis=2)
        vv = jnp.concatenate([jnp.pad(vb, padw)[:, :-1], vb], axis=2)
        qpos = jnp.arange(nb)[:, None, None] * WINDOW + jnp.arange(WINDOW)[None, :, None]
        kpos = jnp.arange(nb)[:, None, None] * WINDOW - WINDOW + jnp.arange(2 * WINDOW)[None, None, :]
        rel = qpos - kpos
        valid = (rel >= 0) & (rel < WINDOW) & (kpos >= 0)
        o_b = _sink_attention(q.reshape(B, nb, WINDOW, KV_B, G_B, HD_B), kk, vv, valid, sinks)
        k_keep = k[:, T - WINDOW:]
        v_keep = v[:, T - WINDOW:]
    else:
        kk = jnp.concatenate([k_buf.astype(k.dtype), k], axis=1)
        vv = jnp.concatenate([v_buf.astype(v.dtype), v], axis=1)
        L = kk.shape[1]
        qpos = (PAST_LEN + jnp.arange(T))[None, :, None]
        kpos = (PAST_LEN - k_buf.shape[1] + jnp.arange(L))[None, None, :]
        rel = qpos - kpos
        valid = (rel >= 0) & (rel < WINDOW)
        o_b = _sink_attention(q.reshape(B, 1, T, KV_B, G_B, HD_B), kk[:, None], vv[:, None], valid, sinks)
        k_keep = kk[:, L - WINDOW:]
        v_keep = vv[:, L - WINDOW:]
    y_b = o_b.reshape(B, T, SWA_Q) @ w_branch_b

    merged = jax.nn.sigmoid(ga + b_merge[:D_MODEL]) * y_a + jax.nn.sigmoid(gb + b_merge[D_MODEL:]) * y_b
    x = _layer_norm(ALPHA * x + merged @ w_out, ln_g, ln_b)
    return x, s_new, k_keep, v_keep


def _moe(x2, w_rg, b_rg, w_re, b_re, w_gate, w_up, w_down):
    n = x2.shape[0]
    lg = (x2 @ w_rg).astype(jnp.float32) + b_rg.astype(jnp.float32)
    grp = jnp.argmax(lg, axis=-1)
    pg = jnp.take_along_axis(jax.nn.softmax(lg, axis=-1), grp[:, None], axis=1)[:, 0]
    le = ((x2 @ w_re).astype(jnp.float32) + b_re.astype(jnp.float32)).reshape(n, N_GROUPS, EXPERTS_PER_GROUP)
    le = jnp.take_along_axis(le, grp[:, None, None], axis=1)[:, 0]
    top_p, top_i = lax.top_k(jax.nn.softmax(le, axis=-1), TOP_K)
    wts = pg[:, None] * top_p / top_p.sum(-1, keepdims=True)
    eid = (grp[:, None] * EXPERTS_PER_GROUP + top_i).astype(jnp.int32).reshape(-1)

    m = n * TOP_K
    tok = jnp.repeat(jnp.arange(n, dtype=jnp.int32), TOP_K)
    se, order = lax.sort((eid, jnp.arange(m, dtype=jnp.int32)), num_keys=1, is_stable=True)
    stok = tok[order]
    sw = wts.reshape(-1)[order]
    counts = jnp.bincount(eid, length=N_EXPERTS).astype(jnp.int32)
    padded = (counts + MOE_BLOCK - 1) // MOE_BLOCK * MOE_BLOCK
    start = jnp.cumsum(counts) - counts
    pend = jnp.cumsum(padded)
    pstart = pend - padded
    dest = pstart[se] + jnp.arange(m, dtype=jnp.int32) - start[se]
    nb = -(-(m + N_EXPERTS * (MOE_BLOCK - 1)) // MOE_BLOCK)
    rows = jnp.zeros((nb * MOE_BLOCK, x2.shape[1]), x2.dtype).at[dest].set(x2[stok])
    blk_e = jnp.minimum(jnp.searchsorted(pend, jnp.arange(nb, dtype=jnp.int32) * MOE_BLOCK, side='right'),
                        N_EXPERTS - 1)

    def expert_block(args):
        xb, e = args
        return (jax.nn.silu(xb @ w_gate[e]) * (xb @ w_up[e])) @ w_down[e]

    y = lax.map(expert_block, (rows.reshape(nb, MOE_BLOCK, -1), blk_e)).reshape(nb * MOE_BLOCK, -1)
    contrib = y[dest] * sw[:, None].astype(y.dtype)
    return jax.ops.segment_sum(contrib, stok, num_segments=n)


def _ffn_layer(x, w_rg, b_rg, w_re, b_re, w_gate, w_up, w_down, ln_g, ln_b):
    B, T, D = x.shape
    out = _moe(x.reshape(B * T, D), w_rg, b_rg, w_re, b_re, w_gate, w_up, w_down).reshape(B, T, D)
    return _layer_norm(ALPHA * x + out, ln_g, ln_b)


def setup_inputs(seed: int = 0) -> dict:
    key = jax.random.key(seed)
    ks = jax.random.split(key, 25)
    f32 = jnp.float32

    def nrm(k, shape, scale):
        return jax.random.normal(k, shape, f32) * scale

    return {
        'x_prompt': nrm(ks[0], (BATCH, SEQ, D_MODEL), 1.0),
        'x_sample': nrm(ks[1], (DEC_BATCH, DEC_SEQ, D_MODEL), 1.0),
        'state_gla': nrm(ks[2], (DEPTH, DEC_BATCH, H_A, DK_A, DV_A), 0.5),
        'cache_swa_k': nrm(ks[3], (DEPTH, DEC_BATCH, WINDOW, KV_B, HD_B), 1.0),
        'cache_swa_v': nrm(ks[4], (DEPTH, DEC_BATCH, WINDOW, KV_B, HD_B), 1.0),
        'w_in': nrm(ks[5], (DEPTH, D_MODEL, N_IN), D_MODEL ** -0.5),
        'w_gf2': nrm(ks[6], (DEPTH, GATE_RANK, GLA_QK), GATE_RANK ** -0.5),
        'b_gf': nrm(ks[7], (DEPTH, GLA_QK), 0.1),
        'gla_norm_g': 1.0 + nrm(ks[8], (DEPTH, H_A, DV_A), 0.01),
        'sinks': nrm(ks[9], (DEPTH, H_B), 0.5),
        'b_merge': nrm(ks[10], (DEPTH, 2 * D_MODEL), 0.02),
        'w_branch_a': nrm(ks[11], (DEPTH, GLA_V, D_MODEL), GLA_V ** -0.5),
        'w_branch_b': nrm(ks[12], (DEPTH, SWA_Q, D_MODEL), SWA_Q ** -0.5),
        'w_out': nrm(ks[13], (DEPTH, D_MODEL, D_MODEL), D_MODEL ** -0.5 * BETA),
        'ln1_g': 1.0 + nrm(ks[14], (DEPTH, D_MODEL), 0.01),
        'ln1_b': nrm(ks[15], (DEPTH, D_MODEL), 0.01),
        'w_router_group': nrm(ks[16], (DEPTH, D_MODEL, N_GROUPS), D_MODEL ** -0.5),
        'b_router_group': nrm(ks[17], (DEPTH, N_GROUPS), 0.01),
        'w_router_expert': nrm(ks[18], (DEPTH, D_MODEL, N_EXPERTS), D_MODEL ** -0.5),
        'b_router_expert': nrm(ks[19], (DEPTH, N_EXPERTS), 0.01),
        'w_gate_e': nrm(ks[20], (DEPTH, N_EXPERTS, D_MODEL, EXPERT_FF), D_MODEL ** -0.5),
        'w_up_e': nrm(ks[21], (DEPTH, N_EXPERTS, D_MODEL, EXPERT_FF), D_MODEL ** -0.5),
        'w_down_e': nrm(ks[22], (DEPTH, N_EXPERTS, EXPERT_FF, D_MODEL), EXPERT_FF ** -0.5 * BETA),
        'ln2_g': 1.0 + nrm(ks[23], (DEPTH, D_MODEL), 0.01),
        'ln2_b': nrm(ks[24], (DEPTH, D_MODEL), 0.01),
    }


def reference(x_prompt, x_sample, state_gla, cache_swa_k, cache_swa_v, w_in, w_gf2, b_gf, gla_norm_g, sinks,
              b_merge, w_branch_a, w_branch_b, w_out, ln1_g, ln1_b, w_router_group, b_router_group,
              w_router_expert, b_router_expert, w_gate_e, w_up_e, w_down_e, ln2_g, ln2_b):
    xp, xs = x_prompt, x_sample
    bp, tp = xp.shape[0], xp.shape[1]
    ts = xs.shape[1]
    pos_p = jnp.arange(tp, dtype=jnp.int32)
    pos_s = PAST_LEN + jnp.arange(ts, dtype=jnp.int32)
    s0p = jnp.zeros((bp, H_A, DK_A, DV_A), jnp.float32)
    g_p, k_p, v_p, g_s, k_s, v_s = [], [], [], [], [], []
    for l in range(DEPTH):
        mix_w = (w_in[l], w_gf2[l], b_gf[l], gla_norm_g[l], sinks[l], b_merge[l], w_branch_a[l],
                 w_branch_b[l], w_out[l], ln1_g[l], ln1_b[l])
        ffn_w = (w_router_group[l], b_router_group[l], w_router_expert[l], b_router_expert[l],
                 w_gate_e[l], w_up_e[l], w_down_e[l], ln2_g[l], ln2_b[l])
        xp, sp, kp, vp = _mixer_layer(xp, pos_p, s0p, None, None, *mix_w)
        xs, ss, ksn, vsn = _mixer_layer(xs, pos_s, state_gla[l], cache_swa_k[l], cache_swa_v[l], *mix_w)
        xp = _ffn_layer(xp, *ffn_w)
        xs = _ffn_layer(xs, *ffn_w)
        g_p.append(sp)
        k_p.append(kp)
        v_p.append(vp)
        g_s.append(ss)
        k_s.append(ksn)
        v_s.append(vsn)
    return (xp, xs, jnp.stack(g_p).astype(state_gla.dtype), jnp.stack(k_p), jnp.stack(v_p),
            jnp.stack(g_s).astype(state_gla.dtype), jnp.stack(k_s), jnp.stack(v_s))
```

```python
import functools
import math

import numpy as np
import jax
import jax.numpy as jnp
from jax import lax
from jax.experimental import pallas as pl
from jax.experimental.pallas import tpu as pltpu

F32 = jnp.float32
BF16 = jnp.bfloat16

D_MODEL = 1024
DEPTH = 4
PAST_LEN = 8192
H_A, DK_A, DV_A = 4, 128, 256
GLA_QK, GLA_V = H_A * DK_A, H_A * DV_A
GATE_RANK = 16
GATE_TAU = 16.0
GLA_CHUNK = 64
HD_B, H_B, KV_B = 64, 16, 4
G_B = H_B // KV_B
SWA_Q, SWA_KV = H_B * HD_B, KV_B * HD_B
WINDOW = 128
ROT_DIM = HD_B // 4
ROPE_THETA = 500000.0
N_GROUPS, EXPERTS_PER_GROUP = 4, 8
N_EXPERTS = N_GROUPS * EXPERTS_PER_GROUP
EXPERT_FF = D_MODEL // 4
ALPHA = (2 * DEPTH) ** 0.25
LN_EPS = 1e-5
NORM_EPS = 1e-6

LANES = 128
SUBLANES = 8
VMEM_LIMIT_BYTES = 56 * 1024 * 1024

ROW_TILE = 256
MOE_BLOCK = 256
GLR_PAD = LANES

C_QK, C_V, C_R, C_SQ, C_KV, C_GAB, C_GLR = 0, 1024, 2048, 3072, 4096, 4608, 6656
N_PROJ = C_GLR + GLR_PAD


def _cparams(sem):
    return pltpu.CompilerParams(dimension_semantics=sem, vmem_limit_bytes=VMEM_LIMIT_BYTES)


def _dot(a, b):
    return jnp.dot(a, b, preferred_element_type=F32)


def _dot_nt(a, b):
    return lax.dot_general(a, b, (((1,), (1,)), ((), ())), preferred_element_type=F32)


def _dot_tn(a, b):
    return lax.dot_general(a, b, (((0,), (0,)), ((), ())), preferred_element_type=F32)


def _layer_norm_rows(y, g, b):
    mu = jnp.mean(y, axis=-1, keepdims=True)
    d = y - mu
    var = jnp.mean(d * d, axis=-1, keepdims=True)
    return d * lax.rsqrt(var + LN_EPS) * g + b


def _inproj_kernel(x_ref, w_ref, qk_ref, v_ref, r_ref, sq_ref, kv_ref, gab_ref, glr_ref):
    xb = x_ref[...].astype(BF16)
    qk_ref[...] = _dot(xb, w_ref[:, C_QK:C_V])
    v_ref[...] = _dot(xb, w_ref[:, C_V:C_R])
    r_ref[...] = _dot(xb, w_ref[:, C_R:C_SQ])
    sq_ref[...] = _dot(xb, w_ref[:, C_SQ:C_KV])
    kv_ref[...] = _dot(xb, w_ref[:, C_KV:C_GAB])
    gab_ref[...] = _dot(xb, w_ref[:, C_GAB:C_GLR])
    glr_ref[...] = _dot(xb, w_ref[:, C_GLR:N_PROJ])


def _inproj(x, w):
    n = x.shape[0]
    widths = (1024, 1024, 1024, 1024, 512, 2048, GLR_PAD)
    return pl.pallas_call(
        _inproj_kernel,
        grid=(n // ROW_TILE,),
        in_specs=[pl.BlockSpec((ROW_TILE, D_MODEL), lambda i: (i, 0)),
                  pl.BlockSpec((D_MODEL, N_PROJ), lambda i: (0, 0))],
        out_specs=[pl.BlockSpec((ROW_TILE, wd), lambda i: (i, 0)) for wd in widths],
        out_shape=[jax.ShapeDtypeStruct((n, wd), F32) for wd in widths],
        compiler_params=_cparams(("parallel",)),
        name="inproj",
    )(x, w)


def _gla_matrices(seq_len):
    c = GLA_CHUNK
    p = np.arange(c)[:, None]
    r = np.arange(c)[None, :]
    same = (p // seq_len) == (r // seq_len)
    mats = [same & (r <= p), same & (r > p)]
    n = seq_len // 2
    while n >= 1:
        mid = (p // (2 * n)) * 2 * n + n - 1
        second = ((p // n) % 2) == 1
        mats.append(np.where(second, (r > mid) & (r <= p), (r > p) & (r <= mid)))
        n //= 2
    return np.concatenate(mats, axis=0).astype(np.float32)


def _log_sigmoid(z):
    return -(jnp.maximum(-z, 0.0) + jnp.log1p(jnp.exp(-jnp.abs(z))))


def _gla_head(q, k, v, glr_b, wgf, bgf, gng, m_ref, states, seq_len):
    c = GLA_CHUNK
    nseq = c // seq_len
    z = _dot(glr_b, wgf) + bgf
    logf = _log_sigmoid(z) * (1.0 / GATE_TAU)
    hi = logf.astype(BF16)
    r1 = logf - hi.astype(F32)
    mid = r1.astype(BF16)
    lo = (r1 - mid.astype(F32)).astype(BF16)
    m = m_ref[...]
    sums = _dot(m, hi) + _dot(m, mid) + _dot(m, lo)
    b = sums[0:c]
    u = sums[c:2 * c]

    q = q * (DK_A ** -0.5)
    rowi = lax.broadcasted_iota(jnp.int32, (c, DK_A), 0)
    ri = lax.broadcasted_iota(jnp.int32, (c, c), 0)
    ci = lax.broadcasted_iota(jnp.int32, (c, c), 1)

    att = jnp.where(ri == ci, jnp.sum(q * k, axis=-1, keepdims=True), 0.0)
    n = seq_len // 2
    lvl = 0
    while n >= 1:
        sh = int(math.log2(n))
        e = jnp.exp(sums[(2 + lvl) * c:(3 + lvl) * c])
        second = ((rowi >> sh) & 1) == 1
        ql = jnp.where(second, q * e, 0.0).astype(BF16)
        kl = jnp.where(second, 0.0, k * e).astype(BF16)
        a = _dot_nt(ql, kl)
        if 2 * n < c:
            a = jnp.where((ri >> (sh + 1)) == (ci >> (sh + 1)), a, 0.0)
        att = att + a
        n //= 2
        lvl += 1

    vb = v.astype(BF16)
    o = _dot(att.astype(BF16), vb)
    q0 = q * jnp.exp(b)
    kd = k * jnp.exp(u)
    new_states = []
    for s in range(nseq):
        if nseq == 1:
            q0s, kds = q0.astype(BF16), kd.astype(BF16)
        else:
            in_seq = (rowi >> int(math.log2(seq_len))) == s
            q0s = jnp.where(in_seq, q0, 0.0).astype(BF16)
            kds = jnp.where(in_seq, kd, 0.0).astype(BF16)
        st = states[s]
        o = o + _dot(q0s, st.astype(BF16))
        last = s * seq_len + seq_len - 1
        ebl = jnp.exp(b[last:last + 1, :])
        col = jnp.transpose(jnp.broadcast_to(ebl, (SUBLANES, DK_A)))[:, 0:1]
        new_states.append(col * st + _dot_tn(kds, vb))
    o = o * lax.rsqrt(jnp.mean(o * o, axis=-1, keepdims=True) + NORM_EPS) * gng
    return o, new_states


def _gla_prompt_kernel(qk_ref, v_ref, glr_ref, wgf_ref, bgf_ref, gng_ref, m_ref,
                       o_ref, sout_ref, s_scr):
    cidx = pl.program_id(1)

    @pl.when(cidx == 0)
    def _():
        s_scr[...] = jnp.zeros_like(s_scr)

    glr_b = glr_ref[...].astype(BF16)
    for h in range(H_A):
        o, (s_new,) = _gla_head(
            qk_ref[:, h * DK_A:(h + 1) * DK_A],
            qk_ref[:, GLA_QK + h * DK_A:GLA_QK + (h + 1) * DK_A],
            v_ref[:, h * DV_A:(h + 1) * DV_A],
            glr_b, wgf_ref[:, h * DK_A:(h + 1) * DK_A], bgf_ref[:, h * DK_A:(h + 1) * DK_A],
            gng_ref[:, h * DV_A:(h + 1) * DV_A], m_ref, [s_scr[h]], GLA_CHUNK)
        o_ref[:, h * DV_A:(h + 1) * DV_A] = o
        s_scr[h] = s_new

    @pl.when(cidx == pl.num_programs(1) - 1)
    def _():
        sout_ref[0] = s_scr[...]


def _gla_prompt(qk, v, glr, wgf, bgf, gng, mats, batch, seq):
    nc = seq // GLA_CHUNK
    row = lambda b, c: (b * nc + c, 0)
    const = lambda b, c: (0, 0)
    return pl.pallas_call(
        _gla_prompt_kernel,
        grid=(batch, nc),
        in_specs=[pl.BlockSpec((GLA_CHUNK, 2 * GLA_QK), row),
                  pl.BlockSpec((GLA_CHUNK, GLA_V), row),
                  pl.BlockSpec((GLA_CHUNK, GLR_PAD), row),
                  pl.BlockSpec((GLR_PAD, GLA_QK), const),
                  pl.BlockSpec((1, GLA_QK), const),
                  pl.BlockSpec((1, GLA_V), const),
                  pl.BlockSpec(mats.shape, const)],
        out_specs=[pl.BlockSpec((GLA_CHUNK, GLA_V), row),
                   pl.BlockSpec((1, H_A, DK_A, DV_A), lambda b, c: (b, 0, 0, 0))],
        out_shape=[jax.ShapeDtypeStruct((batch * seq, GLA_V), F32),
                   jax.ShapeDtypeStruct((batch, H_A, DK_A, DV_A), F32)],
        scratch_shapes=[pltpu.VMEM((H_A, DK_A, DV_A), F32)],
        compiler_params=_cparams(("parallel", "arbitrary")),
        name="gla_prompt",
    )(qk, v, glr, wgf, bgf, gng, mats)


def _gla_sample_kernel(qk_ref, v_ref, glr_ref, wgf_ref, bgf_ref, gng_ref, m_ref, s_ref,
                       o_ref, sout_ref, *, seq_len):
    nseq = GLA_CHUNK // seq_len
    glr_b = glr_ref[...].astype(BF16)
    for h in range(H_A):
        o, s_new = _gla_head(
            qk_ref[:, h * DK_A:(h + 1) * DK_A],
            qk_ref[:, GLA_QK + h * DK_A:GLA_QK + (h + 1) * DK_A],
            v_ref[:, h * DV_A:(h + 1) * DV_A],
            glr_b, wgf_ref[:, h * DK_A:(h + 1) * DK_A], bgf_ref[:, h * DK_A:(h + 1) * DK_A],
            gng_ref[:, h * DV_A:(h + 1) * DV_A], m_ref,
            [s_ref[s, h] for s in range(nseq)], seq_len)
        o_ref[:, h * DV_A:(h + 1) * DV_A] = o
        for s in range(nseq):
            sout_ref[s, h] = s_new[s]


def _gla_sample(qk, v, glr, wgf, bgf, gng, mats, s0, row_block0, batch, seq):
    nseq = GLA_CHUNK // seq
    steps = batch // nseq
    row = lambda i: (row_block0 + i, 0)
    const = lambda i: (0, 0)
    st = lambda i: (i, 0, 0, 0)
    return pl.pallas_call(
        functools.partial(_gla_sample_kernel, seq_len=seq),
        grid=(steps,),
        in_specs=[pl.BlockSpec((GLA_CHUNK, 2 * GLA_QK), row),
                  pl.BlockSpec((GLA_CHUNK, GLA_V), row),
                  pl.BlockSpec((GLA_CHUNK, GLR_PAD), row),
                  pl.BlockSpec((GLR_PAD, GLA_QK), const),
                  pl.BlockSpec((1, GLA_QK), const),
                  pl.BlockSpec((1, GLA_V), const),
                  pl.BlockSpec(mats.shape, const),
                  pl.BlockSpec((nseq, H_A, DK_A, DV_A), st)],
        out_specs=[pl.BlockSpec((GLA_CHUNK, GLA_V), lambda i: (i, 0)),
                   pl.BlockSpec((nseq, H_A, DK_A, DV_A), st)],
        out_shape=[jax.ShapeDtypeStruct((batch * seq, GLA_V), F32),
                   jax.ShapeDtypeStruct((batch, H_A, DK_A, DV_A), F32)],
        compiler_params=_cparams(("parallel",)),
        name="gla_sample",
    )(qk, v, glr, wgf, bgf, gng, mats, s0)


def _rope_tables(pos):
    half = ROT_DIM // 2
    inv = ROPE_THETA ** (-jnp.arange(half, dtype=F32) * 2.0 / ROT_DIM)
    ang = pos.astype(F32)[:, None] * inv[None, :]
    cos, sin = jnp.cos(ang), jnp.sin(ang)
    n = pos.shape[0]
    one = jnp.ones((n, HD_B - ROT_DIM), F32)
    zero = jnp.zeros((n, HD_B - ROT_DIM), F32)
    zh = jnp.zeros((n, half), F32)
    cos_t = jnp.concatenate([cos, cos, one], axis=1)
    sa_t = jnp.concatenate([-sin, zh, zero], axis=1)
    sb_t = jnp.concatenate([zh, sin, zero], axis=1)
    rep = LANES // HD_B
    return (jnp.tile(cos_t, (1, rep)), jnp.tile(sa_t, (1, rep)), jnp.tile(sb_t, (1, rep)))


def _rope(x, cos, sa, sb):
    outs = []
    for c in range(x.shape[1] // LANES):
        xb = x[:, c * LANES:(c + 1) * LANES]
        outs.append(xb * cos + pltpu.roll(xb, LANES - ROT_DIM // 2, 1) * sa
                    + pltpu.roll(xb, ROT_DIM // 2, 1) * sb)
    return outs[0] if len(outs) == 1 else jnp.concatenate(outs, axis=1)


def _swa_attend(q, k_scr, v_scr, sinks_ref, first_key, o_ref, row0, nq):
    kb = k_scr[...].astype(BF16)
    vb = v_scr[...].astype(BF16)
    rows = G_B * nq
    ri = lax.broadcasted_iota(jnp.int32, (rows, 2 * WINDOW), 0)
    ci = lax.broadcasted_iota(jnp.int32, (rows, 2 * WINDOW), 1)
    qi = ri & (nq - 1)
    rel = qi - ci + WINDOW
    valid = (rel >= 0) & (rel < WINDOW) & (ci >= first_key)
    rcol = lax.broadcasted_iota(jnp.int32, (rows, 1), 0) >> int(math.log2(nq))
    for g in range(KV_B):
        qs = jnp.concatenate(
            [q[:, (g * G_B + j) * HD_B:(g * G_B + j + 1) * HD_B] for j in range(G_B)], axis=0)
        qs = (qs * (HD_B ** -0.5)).astype(BF16)
        s = _dot_nt(qs, kb[:, g * HD_B:(g + 1) * HD_B])
        s = jnp.where(valid, s, -jnp.inf)
        sink = jnp.zeros((rows, 1), F32)
        for j in range(G_B):
            sink = jnp.where(rcol == j, sinks_ref[g * G_B + j], sink)
        mx = jnp.maximum(jnp.max(s, axis=-1, keepdims=True), sink)
        p = jnp.exp(s - mx)
        den = jnp.sum(p, axis=-1, keepdims=True) + jnp.exp(sink - mx)
        o = _dot((p / den).astype(BF16), vb[:, g * HD_B:(g + 1) * HD_B])
        for j in range(G_B):
            h = g * G_B + j
            o_ref[pl.ds(row0, nq), h * HD_B:(h + 1) * HD_B] = o[j * nq:(j + 1) * nq]


def _swa_prompt_kernel(sinks_ref, sq_ref, kv_ref, cos_ref, sa_ref, sb_ref,
                       o_ref, kk_ref, vk_ref, k_scr, v_scr):
    w = pl.program_id(1)

    @pl.when(w == 0)
    def _():
        k_scr[0:WINDOW] = jnp.zeros((WINDOW, SWA_KV), F32)
        v_scr[0:WINDOW] = jnp.zeros((WINDOW, SWA_KV), F32)

    cos, sa, sb = cos_ref[...], sa_ref[...], sb_ref[...]
    k_cur = _rope(kv_ref[:, 0:SWA_KV], cos, sa, sb)
    v_cur = kv_ref[:, SWA_KV:2 * SWA_KV]
    k_scr[WINDOW:2 * WINDOW] = k_cur
    v_scr[WINDOW:2 * WINDOW] = v_cur
    q = _rope(sq_ref[...], cos, sa, sb)
    _swa_attend(q, k_scr, v_scr, sinks_ref, jnp.where(w > 0, 0, WINDOW), o_ref, 0, WINDOW)
    kk_ref[0] = k_cur
    vk_ref[0] = v_cur
    k_scr[0:WINDOW] = k_cur
    v_scr[0:WINDOW] = v_cur


def _swa_prompt(sinks, sq, kv, tabs, batch, seq):
    nw = seq // WINDOW
    row = lambda b, w: (b * nw + w, 0)
    tab = lambda b, w: (w, 0)
    keep = lambda b, w: (b, 0, 0)
    return pl.pallas_call(
        _swa_prompt_kernel,
        grid=(batch, nw),
        in_specs=[pl.BlockSpec(memory_space=pltpu.SMEM),
                  pl.BlockSpec((WINDOW, SWA_Q), row),
                  pl.BlockSpec((WINDOW, 2 * SWA_KV), row),
                  pl.BlockSpec((WINDOW, LANES), tab),
                  pl.BlockSpec((WINDOW, LANES), tab),
                  pl.BlockSpec((WINDOW, LANES), tab)],
        out_specs=[pl.BlockSpec((WINDOW, SWA_Q), row),
                   pl.BlockSpec((1, WINDOW, SWA_KV), keep),
                   pl.BlockSpec((1, WINDOW, SWA_KV), keep)],
        out_shape=[jax.ShapeDtypeStruct((batch * seq, SWA_Q), F32),
                   jax.ShapeDtypeStruct((batch, WINDOW, SWA_KV), F32),
                   jax.ShapeDtypeStruct((batch, WINDOW, SWA_KV), F32)],
        scratch_shapes=[pltpu.VMEM((2 * WINDOW, SWA_KV), F32),
                        pltpu.VMEM((2 * WINDOW, SWA_KV), F32)],
        compiler_params=_cparams(("parallel", "arbitrary")),
        name="swa_prompt",
    )(sinks, sq, kv, *tabs)


SWA_SAMPLE_SEQS = 8


def _swa_sample_kernel(sinks_ref, sq_ref, kv_ref, cos_ref, sa_ref, sb_ref, ck_ref, cv_ref,
                       o_ref, kk_ref, vk_ref, k_scr, v_scr, *, seq):
    k_scr[WINDOW:2 * WINDOW] = jnp.zeros((WINDOW, SWA_KV), F32)
    v_scr[WINDOW:2 * WINDOW] = jnp.zeros((WINDOW, SWA_KV), F32)
    cos, sa, sb = cos_ref[...], sa_ref[...], sb_ref[...]

    def body(i, carry):
        r0 = pl.multiple_of(i * seq, seq)
        k_new = _rope(kv_ref[pl.ds(r0, seq), 0:SWA_KV], cos, sa, sb)
        v_new = kv_ref[pl.ds(r0, seq), SWA_KV:2 * SWA_KV]
        k_scr[0:WINDOW] = ck_ref[i]
        v_scr[0:WINDOW] = cv_ref[i]
        k_scr[WINDOW:WINDOW + seq] = k_new
        v_scr[WINDOW:WINDOW + seq] = v_new
        q = _rope(sq_ref[pl.ds(r0, seq), :], cos, sa, sb)
        _swa_attend(q, k_scr, v_scr, sinks_ref, 0, o_ref, r0, seq)
        kk_ref[i] = k_scr[seq:seq + WINDOW]
        vk_ref[i] = v_scr[seq:seq + WINDOW]
        return carry

    lax.fori_loop(0, SWA_SAMPLE_SEQS, body, 0)


def _swa_sample(sinks, sq, kv, tabs, cache_k, cache_v, row_block0, batch, seq):
    nb = SWA_SAMPLE_SEQS
    rows = nb * seq
    row = lambda i: (row_block0 + i, 0)
    tab = lambda i: (0, 0)
    cache = lambda i: (i, 0, 0)
    return pl.pallas_call(
        functools.partial(_swa_sample_kernel, seq=seq),
        grid=(batch // nb,),
        in_specs=[pl.BlockSpec(memory_space=pltpu.SMEM),
                  pl.BlockSpec((rows, SWA_Q), row),
                  pl.BlockSpec((rows, 2 * SWA_KV), row),
                  pl.BlockSpec((seq, LANES), tab),
                  pl.BlockSpec((seq, LANES), tab),
                  pl.BlockSpec((seq, LANES), tab),
                  pl.BlockSpec((nb, WINDOW, SWA_KV), cache),
                  pl.BlockSpec((nb, WINDOW, SWA_KV), cache)],
        out_specs=[pl.BlockSpec((rows, SWA_Q), lambda i: (i, 0)),
                   pl.BlockSpec((nb, WINDOW, SWA_KV), cache),
                   pl.BlockSpec((nb, WINDOW, SWA_KV), cache)],
        out_shape=[jax.ShapeDtypeStruct((batch * seq, SWA_Q), F32),
                   jax.ShapeDtypeStruct((batch, WINDOW, SWA_KV), F32),
                   jax.ShapeDtypeStruct((batch, WINDOW, SWA_KV), F32)],
        scratch_shapes=[pltpu.VMEM((2 * WINDOW, SWA_KV), F32),
                        pltpu.VMEM((2 * WINDOW, SWA_KV), F32)],
        compiler_params=_cparams(("arbitrary",)),
        name="swa_sample",
    )(sinks, sq, kv, *tabs, cache_k, cache_v)


def _merge_kernel(oa_ref, r_ref, ob_ref, gab_ref, x_ref, wa_ref, wb_ref, wo_ref,
                  bm_ref, g_ref, b_ref, y_ref):
    r = r_ref[...]
    a = (oa_ref[...] * (r * jax.nn.sigmoid(r))).astype(BF16)
    ya = _dot(a, wa_ref[...])
    yb = _dot(ob_ref[...].astype(BF16), wb_ref[...])
    ga = jax.nn.sigmoid(gab_ref[:, 0:D_MODEL] + bm_ref[:, 0:D_MODEL])
    gb = jax.nn.sigmoid(gab_ref[:, D_MODEL:2 * D_MODEL] + bm_ref[:, D_MODEL:2 * D_MODEL])
    merged = (ga * ya + gb * yb).astype(BF16)
    y = ALPHA * x_ref[...] + _dot(merged, wo_ref[...])
    y_ref[...] = _layer_norm_rows(y, g_ref[...], b_ref[...])


def _merge(oa, r, ob, gab, x, wa, wb, wo, bm, g, b):
    n = x.shape[0]
    row = lambda w: pl.BlockSpec((ROW_TILE, w), lambda i: (i, 0))
    const = lambda s: pl.BlockSpec(s, lambda i: (0, 0))
    return pl.pallas_call(
        _merge_kernel,
        grid=(n // ROW_TILE,),
        in_specs=[row(GLA_V), row(GLA_V), row(SWA_Q), row(2 * D_MODEL), row(D_MODEL),
                  const((GLA_V, D_MODEL)), const((SWA_Q, D_MODEL)), const((D_MODEL, D_MODEL)),
                  const((1, 2 * D_MODEL)), const((1, D_MODEL)), const((1, D_MODEL))],
        out_specs=row(D_MODEL),
        out_shape=jax.ShapeDtypeStruct((n, D_MODEL), F32),
        compiler_params=_cparams(("parallel",)),
        name="merge",
    )(oa, r, ob, gab, x, wa, wb, wo, bm, g, b)


ROUTE_LANES = LANES
R_E0, R_E1, R_W0, R_W1, R_RANK0, R_RANK1 = 0, 1, 2, 3, 4, 5


def _router_kernel(x_ref, w_ref, b_ref, tri_ref, route_ref, counts_ref, carry):
    i = pl.program_id(0)

    @pl.when(i == 0)
    def _():
        carry[...] = jnp.zeros_like(carry)

    logits = jnp.dot(x_ref[...], w_ref[...], preferred_element_type=F32,
                     precision=lax.Precision.HIGHEST) + b_ref[...]
    t = logits.shape[0]
    lane = lax.broadcasted_iota(jnp.int32, (t, ROUTE_LANES), 1).astype(F32)
    neg = -jnp.inf
    big = float(ROUTE_LANES)

    def first_argmax(vals, mask):
        mv = jnp.where(mask, vals, neg)
        mx = jnp.max(mv, axis=-1, keepdims=True)
        idx = jnp.min(jnp.where(mask & (mv == mx), lane, big), axis=-1, keepdims=True)
        return mx, idx

    gmask = lane < N_GROUPS
    gmax, grp = first_argmax(logits, gmask)
    pg = 1.0 / jnp.sum(jnp.where(gmask, jnp.exp(logits - gmax), 0.0), axis=-1, keepdims=True)
    lo = N_GROUPS + grp * EXPERTS_PER_GROUP
    emask = (lane >= lo) & (lane < lo + EXPERTS_PER_GROUP)
    m0, i0 = first_argmax(logits, emask)
    m1, i1 = first_argmax(logits, emask & (lane != i0))
    r = jnp.exp(m1 - m0)
    w0 = pg / (1.0 + r)
    w1 = pg * r / (1.0 + r)
    e0 = i0 - N_GROUPS
    e1 = i1 - N_GROUPS

    oh0 = lane == e0
    oh1 = lane == e1
    both = jnp.where(oh0 | oh1, 1.0, 0.0)
    before = _dot(tri_ref[...], both.astype(BF16)) + carry[0:1, :]
    rank0 = jnp.sum(jnp.where(oh0, before, 0.0), axis=-1, keepdims=True)
    rank1 = jnp.sum(jnp.where(oh1, before, 0.0), axis=-1, keepdims=True)
    carry[...] = carry[...] + jnp.sum(both, axis=0, keepdims=True)

    rec = jnp.zeros((t, ROUTE_LANES), F32)
    for ln, val in ((R_E0, e0), (R_E1, e1), (R_W0, w0), (R_W1, w1),
                    (R_RANK0, rank0), (R_RANK1, rank1)):
        rec = jnp.where(lane == ln, val, rec)
    route_ref[...] = rec
    counts_ref[...] = carry[...]


def _router(x, w, b, tri):
    n = x.shape[0]
    const = lambda s: pl.BlockSpec(s, lambda i: (0, 0))
    return pl.pallas_call(
        _router_kernel,
        grid=(n // ROW_TILE,),
        in_specs=[pl.BlockSpec((ROW_TILE, D_MODEL), lambda i: (i, 0)),
                  const((D_MODEL, ROUTE_LANES)), const((1, ROUTE_LANES)),
                  const((ROW_TILE, ROW_TILE))],
        out_specs=[pl.BlockSpec((ROW_TILE, ROUTE_LANES), lambda i: (i, 0)),
                   const((SUBLANES, ROUTE_LANES))],
        out_shape=[jax.ShapeDtypeStruct((n, ROUTE_LANES), F32),
                   jax.ShapeDtypeStruct((SUBLANES, ROUTE_LANES), F32)],
        scratch_shapes=[pltpu.VMEM((SUBLANES, ROUTE_LANES), F32)],
        compiler_params=_cparams(("arbitrary",)),
        name="router",
    )(x, w, b, tri)


def _row_copy(src_hbm, src_row, dst_hbm, dst_row, sem):
    return pltpu.make_async_copy(src_hbm.at[pl.ds(src_row, 1)], dst_hbm.at[pl.ds(dst_row, 1)], sem)


def _dispatch_kernel(d0_ref, d1_ref, x_hbm, rows_in_hbm, rows_hbm, sem):
    del rows_in_hbm
    base = pl.program_id(0) * ROW_TILE

    def start(t, carry):
        _row_copy(x_hbm, base + t, rows_hbm, d0_ref[0, 0, t], sem).start()
        _row_copy(x_hbm, base + t, rows_hbm, d1_ref[0, 0, t], sem).start()
        return carry

    lax.fori_loop(0, ROW_TILE, start, 0)

    def wait(t, carry):
        _row_copy(x_hbm, 0, rows_hbm, 0, sem).wait()
        _row_copy(x_hbm, 0, rows_hbm, 0, sem).wait()
        return carry

    lax.fori_loop(0, ROW_TILE, wait, 0)


def _dispatch(d0, d1, x, rows_zero):
    n = x.shape[0]
    idx = pl.BlockSpec((1, 1, ROW_TILE), lambda i: (i, 0, 0), memory_space=pltpu.SMEM)
    any_ = pl.BlockSpec(memory_space=pl.ANY)
    return pl.pallas_call(
        _dispatch_kernel,
        grid=(n // ROW_TILE,),
        in_specs=[idx, idx, any_, any_],
        out_specs=any_,
        out_shape=jax.ShapeDtypeStruct(rows_zero.shape, F32),
        scratch_shapes=[pltpu.SemaphoreType.DMA(())],
        input_output_aliases={3: 0},
        compiler_params=_cparams(("arbitrary",)),
        name="moe_dispatch",
    )(d0.reshape(n // ROW_TILE, 1, ROW_TILE), d1.reshape(n // ROW_TILE, 1, ROW_TILE), x, rows_zero)


def _expert_kernel(be_ref, nu_ref, rows_ref, wg_ref, wu_ref, wd_ref, y_ref):
    del be_ref

    used = pl.program_id(0) < nu_ref[0]

    @pl.when(used)
    def _():
        xb = rows_ref[...].astype(BF16)
        g = _dot(xb, wg_ref[0])
        u = _dot(xb, wu_ref[0])
        h = (g * jax.nn.sigmoid(g) * u).astype(BF16)
        y_ref[...] = _dot(h, wd_ref[0])

    @pl.when(jnp.logical_not(used))
    def _():
        y_ref[...] = jnp.zeros_like(y_ref)


def _experts(blk_e, n_used, rows, wg, wu, wd):
    nb = rows.shape[0] // MOE_BLOCK
    rblk = lambda i, be, nu: (jnp.minimum(i, nu[0] - 1), 0)
    wsel = lambda i, be, nu: (be[i], 0, 0)
    gs = pltpu.PrefetchScalarGridSpec(
        num_scalar_prefetch=2,
        grid=(nb,),
        in_specs=[pl.BlockSpec((MOE_BLOCK, D_MODEL), rblk),
                  pl.BlockSpec((1, D_MODEL, EXPERT_FF), wsel),
                  pl.BlockSpec((1, D_MODEL, EXPERT_FF), wsel),
                  pl.BlockSpec((1, EXPERT_FF, D_MODEL), wsel)],
        out_specs=pl.BlockSpec((MOE_BLOCK, D_MODEL), lambda i, be, nu: (i, 0)))
    return pl.pallas_call(
        _expert_kernel,
        grid_spec=gs,
        out_shape=jax.ShapeDtypeStruct(rows.shape, F32),
        compiler_params=_cparams(("arbitrary",)),
        name="moe_experts",
    )(blk_e, n_used, rows, wg, wu, wd)


def _combine_kernel(d0_ref, d1_ref, y_hbm, route_ref, x_ref, g_ref, b_ref, o_ref, buf, sem):
    def start(t, carry):
        pltpu.make_async_copy(y_hbm.at[pl.ds(d0_ref[0, 0, t], 1)], buf.at[0, pl.ds(t, 1)], sem).start()
        pltpu.make_async_copy(y_hbm.at[pl.ds(d1_ref[0, 0, t], 1)], buf.at[1, pl.ds(t, 1)], sem).start()
        return carry

    lax.fori_loop(0, ROW_TILE, start, 0)

    def wait(t, carry):
        pltpu.make_async_copy(y_hbm.at[pl.ds(0, 1)], buf.at[0, pl.ds(0, 1)], sem).wait()
        pltpu.make_async_copy(y_hbm.at[pl.ds(0, 1)], buf.at[1, pl.ds(0, 1)], sem).wait()
        return carry

    lax.fori_loop(0, ROW_TILE, wait, 0)

    w0 = route_ref[:, R_W0:R_W0 + 1]
    w1 = route_ref[:, R_W1:R_W1 + 1]
    moe = w0 * buf[0] + w1 * buf[1]
    o_ref[...] = _layer_norm_rows(ALPHA * x_ref[...] + moe, g_ref[...], b_ref[...])


def _combine(d0, d1, y, route, x, g, b):
    n = x.shape[0]
    idx = pl.BlockSpec((1, 1, ROW_TILE), lambda i: (i, 0, 0), memory_space=pltpu.SMEM)
    row = lambda w: pl.BlockSpec((ROW_TILE, w), lambda i: (i, 0))
    const = lambda s: pl.BlockSpec(s, lambda i: (0, 0))
    return pl.pallas_call(
        _combine_kernel,
        grid=(n // ROW_TILE,),
        in_specs=[idx, idx, pl.BlockSpec(memory_space=pl.ANY), row(ROUTE_LANES), row(D_MODEL),
                  const((1, D_MODEL)), const((1, D_MODEL))],
        out_specs=row(D_MODEL),
        out_shape=jax.ShapeDtypeStruct((n, D_MODEL), F32),
        scratch_shapes=[pltpu.VMEM((2, ROW_TILE, D_MODEL), F32), pltpu.SemaphoreType.DMA(())],
        compiler_params=_cparams(("arbitrary",)),
        name="moe_combine",
    )(d0.reshape(n // ROW_TILE, 1, ROW_TILE), d1.reshape(n // ROW_TILE, 1, ROW_TILE), y, route, x, g, b)


def _moe_layer(x, wr, br, tri, wg, wu, wd, g, b):
    n = x.shape[0]
    route, counts = _router(x, wr, br, tri)
    cnt = counts[0, :N_EXPERTS].astype(jnp.int32)
    padded = (cnt + MOE_BLOCK - 1) // MOE_BLOCK * MOE_BLOCK
    pend = jnp.cumsum(padded)
    pstart = pend - padded
    e0 = route[:, R_E0].astype(jnp.int32)
    e1 = route[:, R_E1].astype(jnp.int32)
    d0 = pstart[e0] + route[:, R_RANK0].astype(jnp.int32)
    d1 = pstart[e1] + route[:, R_RANK1].astype(jnp.int32)
    nb = (2 * n + N_EXPERTS * (MOE_BLOCK - 1)) // MOE_BLOCK
    blk_e = jnp.minimum(
        jnp.searchsorted(pend, jnp.arange(nb, dtype=jnp.int32) * MOE_BLOCK, side='right'),
        N_EXPERTS - 1).astype(jnp.int32)
    n_used = (pend[-1:] // MOE_BLOCK).astype(jnp.int32)
    rows = _dispatch(d0, d1, x, jnp.zeros((nb * MOE_BLOCK, D_MODEL), F32))
    y = _experts(blk_e, n_used, rows, wg, wu, wd)
    return _combine(d0, d1, y, route, x, g, b)


def _prep_weights(w_in, w_gf2, w_router_group, b_router_group, w_router_expert, b_router_expert):
    zpad = jnp.zeros((DEPTH, D_MODEL, GLR_PAD - GATE_RANK), F32)
    w_proj = jnp.concatenate([w_in[:, :, :3072], w_in[:, :, 3088:], w_in[:, :, 3072:3088], zpad],
                             axis=2).astype(BF16)
    wgf = jnp.concatenate([w_gf2, jnp.zeros((DEPTH, GLR_PAD - GATE_RANK, GLA_QK), F32)],
                          axis=1).astype(BF16)
    rpad = ROUTE_LANES - N_GROUPS - N_EXPERTS
    wr = jnp.concatenate([w_router_group, w_router_expert,
                          jnp.zeros((DEPTH, D_MODEL, rpad), F32)], axis=2)
    br = jnp.concatenate([b_router_group, b_router_expert, jnp.zeros((DEPTH, rpad), F32)], axis=1)
    return w_proj, wgf, wr, br[:, None, :]


def kernel(x_prompt, x_sample, state_gla, cache_swa_k, cache_swa_v, w_in, w_gf2, b_gf, gla_norm_g, sinks,
           b_merge, w_branch_a, w_branch_b, w_out, ln1_g, ln1_b, w_router_group, b_router_group,
           w_router_expert, b_router_expert, w_gate_e, w_up_e, w_down_e, ln2_g, ln2_b):
    bp, tp, _ = x_prompt.shape
    bs, ts, _ = x_sample.shape
    n_p, n_s = bp * tp, bs * ts
    x = jnp.concatenate([x_prompt.reshape(n_p, D_MODEL), x_sample.reshape(n_s, D_MODEL)], axis=0)

    w_proj, wgf, wr, br = _prep_weights(w_in, w_gf2, w_router_group, b_router_group,
                                        w_router_expert, b_router_expert)
    wa, wb, wo = w_branch_a.astype(BF16), w_branch_b.astype(BF16), w_out.astype(BF16)
    wg, wu, wd = w_gate_e.astype(BF16), w_up_e.astype(BF16), w_down_e.astype(BF16)
    mats_p = jnp.asarray(_gla_matrices(GLA_CHUNK), BF16)
    mats_s = jnp.asarray(_gla_matrices(ts), BF16)
    tabs_p = _rope_tables(jnp.arange(tp, dtype=jnp.int32))
    tabs_s = _rope_tables(PAST_LEN + jnp.arange(ts, dtype=jnp.int32))
    tri = jnp.asarray(np.tril(np.ones((ROW_TILE, ROW_TILE), np.float32), -1), BF16)
    cache_k = cache_swa_k.reshape(DEPTH, bs, WINDOW, SWA_KV)
    cache_v = cache_swa_v.reshape(DEPTH, bs, WINDOW, SWA_KV)

    outs = [[] for _ in range(6)]
    for l in range(DEPTH):
        qk, v, r, sq, kv, gab, glr = _inproj(x, w_proj[l])
        bgf, gng = b_gf[l][None, :], gla_norm_g[l].reshape(1, GLA_V)
        oa_p, st_p = _gla_prompt(qk, v, glr, wgf[l], bgf, gng, mats_p, bp, tp)
        oa_s, st_s = _gla_sample(qk, v, glr, wgf[l], bgf, gng, mats_s, state_gla[l],
                                 n_p // GLA_CHUNK, bs, ts)
        ob_p, kk_p, vk_p = _swa_prompt(sinks[l], sq, kv, tabs_p, bp, tp)
        ob_s, kk_s, vk_s = _swa_sample(sinks[l], sq, kv, tabs_s, cache_k[l], cache_v[l],
                                       n_p // (SWA_SAMPLE_SEQS * ts), bs, ts)
        oa = jnp.concatenate([oa_p, oa_s], axis=0)
        ob = jnp.concatenate([ob_p, ob_s], axis=0)
        x = _merge(oa, r, ob, gab, x, wa[l], wb[l], wo[l], b_merge[l][None, :],
                   ln1_g[l][None, :], ln1_b[l][None, :])
        x = _moe_layer(x, wr[l], br[l], tri, wg[l], wu[l], wd[l], ln2_g[l][None, :], ln2_b[l][None, :])
        for lst, val in zip(outs, (st_p, kk_p, vk_p, st_s, kk_s, vk_s)):
            lst.append(val)

    st_p, kk_p, vk_p, st_s, kk_s, vk_s = [jnp.stack(o) for o in outs]
    return (x[:n_p].reshape(bp, tp, D_MODEL), x[n_p:].reshape(bs, ts, D_MODEL),
            st_p.astype(state_gla.dtype),
            kk_p.reshape(DEPTH, bp, WINDOW, KV_B, HD_B), vk_p.reshape(DEPTH, bp, WINDOW, KV_B, HD_B),
            st_s.astype(state_gla.dtype),
            kk_s.reshape(DEPTH, bs, WINDOW, KV_B, HD_B), vk_s.reshape(DEPTH, bs, WINDOW, KV_B, HD_B))
```

```python
import functools
import math

import numpy as np
import jax
import jax.numpy as jnp
from jax import lax
from jax.experimental import pallas as pl
from jax.experimental.pallas import tpu as pltpu

F32 = jnp.float32
BF16 = jnp.bfloat16

D_MODEL = 1024
DEPTH = 4
PAST_LEN = 8192
H_A, DK_A, DV_A = 4, 128, 256
GLA_QK, GLA_V = H_A * DK_A, H_A * DV_A
GATE_RANK = 16
GATE_TAU = 16.0
GLA_CHUNK = 64
HD_B, H_B, KV_B = 64, 16, 4
G_B = H_B // KV_B
SWA_Q, SWA_KV = H_B * HD_B, KV_B * HD_B
WINDOW = 128
ROT_DIM = HD_B // 4
ROPE_THETA = 500000.0
N_GROUPS, EXPERTS_PER_GROUP = 4, 8
N_EXPERTS = N_GROUPS * EXPERTS_PER_GROUP
EXPERT_FF = D_MODEL // 4
ALPHA = (2 * DEPTH) ** 0.25
LN_EPS = 1e-5
NORM_EPS = 1e-6

LANES = 128
SUBLANES = 8
VMEM_LIMIT_BYTES = 56 * 1024 * 1024

ROW_TILE = 256
MOE_BLOCK = 512
GLR_PAD = LANES
GLA_STEP_CHUNKS = ROW_TILE // GLA_CHUNK
SWA_SAMPLE_SEQS = 8

C_QK, C_V, C_R = 0, GLA_QK * 2, GLA_QK * 2 + GLA_V
W_IN_GLR = C_R + GLA_V
C_SQ = W_IN_GLR
C_KV, C_GAB = C_SQ + SWA_Q, C_SQ + SWA_Q + 2 * SWA_KV
C_GLR = C_GAB + 2 * D_MODEL
N_PROJ = C_GLR + GLR_PAD


def _cparams(sem):
    return pltpu.CompilerParams(dimension_semantics=sem, vmem_limit_bytes=VMEM_LIMIT_BYTES)


def _dot(a, b):
    return jnp.dot(a, b, preferred_element_type=F32)


def _dot_nt(a, b):
    return lax.dot_general(a, b, (((1,), (1,)), ((), ())), preferred_element_type=F32)


def _dot_tn(a, b):
    return lax.dot_general(a, b, (((0,), (0,)), ((), ())), preferred_element_type=F32)


def _layer_norm_rows(y, g, b):
    mu = jnp.mean(y, axis=-1, keepdims=True)
    d = y - mu
    var = jnp.mean(d * d, axis=-1, keepdims=True)
    return d * lax.rsqrt(var + LN_EPS) * g + b


PACKED = D_MODEL // 2


def _pack_rows(x):
    lo = pltpu.bitcast(x[:, :PACKED].astype(jnp.bfloat16).astype(F32), jnp.uint32)
    hi = pltpu.bitcast(x[:, PACKED:].astype(jnp.bfloat16).astype(F32), jnp.uint32)
    return hi | (lo >> 16)


def _unpack_rows(w):
    lo = pltpu.bitcast(w << 16, F32)
    hi = pltpu.bitcast(w & jnp.uint32(0xFFFF0000), F32)
    return jnp.concatenate([lo, hi], axis=1)


PROJ_WIDTHS = (1024, 1024, 1024, 1024, 512, 2048, GLR_PAD)
def _proj_dtypes():
    return (F32, BF16, BF16, BF16, F32, BF16, F32)


def _inproj_tile(x, w_ref, out_refs):
    xb = x.astype(BF16)
    col = 0
    for ref, wd in zip(out_refs, PROJ_WIDTHS):
        ref[...] = _dot(xb, w_ref[:, col:col + wd]).astype(ref.dtype)
        col += wd


def _inproj_kernel(x_ref, w_ref, *out_refs):
    _inproj_tile(x_ref[...], w_ref, out_refs)


def _inproj(x, w, layer):
    n = x.shape[0]
    widths, dtypes = PROJ_WIDTHS, _proj_dtypes()
    return pl.pallas_call(
        _inproj_kernel,
        grid=(n // ROW_TILE,),
        in_specs=[pl.BlockSpec((ROW_TILE, D_MODEL), lambda i: (i, 0)),
                  pl.BlockSpec((None, D_MODEL, N_PROJ), lambda i: (layer, 0, 0))],
        out_specs=[pl.BlockSpec((ROW_TILE, wd), lambda i: (i, 0)) for wd in widths],
        out_shape=[jax.ShapeDtypeStruct((n, wd), dt) for wd, dt in zip(widths, dtypes)],
        compiler_params=_cparams(("parallel",)),
        name="inproj",
    )(x, w)


def _gla_matrices(seq_len):
    c = GLA_CHUNK
    p = np.arange(c)[:, None]
    r = np.arange(c)[None, :]
    same = (p // seq_len) == (r // seq_len)
    return np.concatenate([same & (r <= p), same & (r > p)], axis=0).astype(np.float32)


def _log_sigmoid(z):
    return -(jnp.maximum(-z, 0.0) + jnp.log(1.0 + jnp.exp(-jnp.abs(z))))


def _level_exponents(b, logf, n, rowi):
    c = b.shape[0]
    if n == 1:
        return jnp.where((rowi & 1) == 1, logf, 0.0)
    pieces = []
    if n >= SUBLANES:
        for j in range(c // (2 * n)):
            mid = j * 2 * n + n - 1
            pieces.append(jnp.broadcast_to(b[mid:mid + 1, :], (2 * n, DK_A)))
    else:
        sub = rowi[0:SUBLANES]
        for g in range(c // SUBLANES):
            bg = b[g * SUBLANES:(g + 1) * SUBLANES]
            piece = None
            for j in range(SUBLANES // (2 * n)):
                mid = j * 2 * n + n - 1
                bc = jnp.broadcast_to(bg[mid:mid + 1, :], (SUBLANES, DK_A))
                piece = bc if piece is None else jnp.where(sub >= j * 2 * n, bc, piece)
            pieces.append(piece)
    bmid = jnp.concatenate(pieces, axis=0)
    second = ((rowi >> int(math.log2(n))) & 1) == 1
    return jnp.where(second, b - bmid, bmid - b)


def _no_fill():
    pass


def _gla_local(probs, m_ref, seq_len, fill=_no_fill):
    c = GLA_CHUNK
    nseq = c // seq_len
    m = m_ref[...]
    rowi = lax.broadcasted_iota(jnp.int32, (c, DK_A), 0)
    ri = lax.broadcasted_iota(jnp.int32, (c, c), 0)
    ci = lax.broadcasted_iota(jnp.int32, (c, c), 1)

    zs = [_dot(glr_b, wgf) + bgf for (_, _, _, glr_b, wgf, bgf) in probs]
    fill()
    logfs = [_log_sigmoid(z) * (1.0 / GATE_TAU) for z in zs]
    sums = []
    for logf in logfs:
        hi = logf.astype(BF16)
        r1 = logf - hi.astype(F32)
        mid = r1.astype(BF16)
        lo = (r1 - mid.astype(F32)).astype(BF16)
        sums.append(_dot(m, hi) + _dot(m, mid) + _dot(m, lo))
    fill()

    qs = [p[0] * (DK_A ** -0.5) for p in probs]
    ks = [p[1] for p in probs]
    atts = [jnp.where(ri == ci, jnp.sum(q * k, axis=-1, keepdims=True), 0.0)
            for q, k in zip(qs, ks)]
    n = seq_len // 2
    while n >= 1:
        sh = int(math.log2(n))
        second = ((rowi >> sh) & 1) == 1
        for i, (q, k) in enumerate(zip(qs, ks)):
            e = jnp.exp(_level_exponents(sums[i][0:c], logfs[i], n, rowi))
            ql = jnp.where(second, q * e, 0.0).astype(BF16)
            kl = jnp.where(second, 0.0, k * e).astype(BF16)
            a = _dot_nt(ql, kl)
            if 2 * n < c:
                a = jnp.where((ri >> (sh + 1)) == (ci >> (sh + 1)), a, 0.0)
            atts[i] = atts[i] + a
        fill()
        n //= 2

    vbs = [p[2].astype(BF16) for p in probs]
    o_intra = [_dot(att.astype(BF16), vb) for att, vb in zip(atts, vbs)]
    q0s = [q * jnp.exp(sm[0:c]) for q, sm in zip(qs, sums)]
    kds = [k * jnp.exp(sm[c:2 * c]) for k, sm in zip(ks, sums)]
    out = []
    for i in range(len(probs)):
        per_seq = []
        for s in range(nseq):
            if nseq == 1:
                q0, kd = q0s[i].astype(BF16), kds[i].astype(BF16)
            else:
                in_seq = (rowi >> int(math.log2(seq_len))) == s
                q0 = jnp.where(in_seq, q0s[i], 0.0).astype(BF16)
                kd = jnp.where(in_seq, kds[i], 0.0).astype(BF16)
            last = s * seq_len + seq_len - 1
            ebl = jnp.exp(sums[i][last:last + 1, :])
            col = jnp.transpose(jnp.broadcast_to(ebl, (SUBLANES, DK_A)))[:, 0:1]
            per_seq.append((q0, _dot_tn(kd, vbs[i]), col))
        out.append((o_intra[i], per_seq))
    return out


def _gla_norm(o, gng):
    return o * lax.rsqrt(jnp.mean(o * o, axis=-1, keepdims=True) + NORM_EPS) * gng


def _gla_tile(qk_ref, v_ref, glr_ref, wgf_ref, bgf_ref, gng_ref, m_ref, s_scr, o_ref, fill=_no_fill):
    probs, where = [], []
    for c in range(GLA_STEP_CHUNKS):
        rows = slice(c * GLA_CHUNK, (c + 1) * GLA_CHUNK)
        glr_b = glr_ref[rows, :].astype(BF16)
        for h in range(H_A):
            probs.append((qk_ref[rows, h * DK_A:(h + 1) * DK_A],
                          qk_ref[rows, GLA_QK + h * DK_A:GLA_QK + (h + 1) * DK_A],
                          v_ref[rows, h * DV_A:(h + 1) * DV_A], glr_b,
                          wgf_ref[:, h * DK_A:(h + 1) * DK_A], bgf_ref[:, h * DK_A:(h + 1) * DK_A]))
            where.append((rows, h))
    local = _gla_local(probs, m_ref, GLA_CHUNK, fill)
    outs = []
    for (rows, h), (o_intra, ((q0, upd, col),)) in zip(where, local):
        st = s_scr[h]
        outs.append(o_intra + _dot(q0, st.astype(BF16)))
        s_scr[h] = col * st + upd
    for (rows, h), o in zip(where, outs):
        o_ref[rows, h * DV_A:(h + 1) * DV_A] = _gla_norm(
            o, gng_ref[:, h * DV_A:(h + 1) * DV_A]).astype(o_ref.dtype)


def _gla_sample_kernel(qk_ref, v_ref, glr_ref, wgf_ref, bgf_ref, gng_ref, m_ref, s_ref, all_in,
                       o_ref, sout_ref, *, seq_len):
    del all_in
    nseq = GLA_CHUNK // seq_len
    glr_b = glr_ref[...].astype(BF16)
    probs = [(qk_ref[:, h * DK_A:(h + 1) * DK_A],
              qk_ref[:, GLA_QK + h * DK_A:GLA_QK + (h + 1) * DK_A],
              v_ref[:, h * DV_A:(h + 1) * DV_A], glr_b,
              wgf_ref[:, h * DK_A:(h + 1) * DK_A], bgf_ref[:, h * DK_A:(h + 1) * DK_A])
             for h in range(H_A)]
    local = _gla_local(probs, m_ref, seq_len)
    outs = []
    for h, (o, per_seq) in enumerate(local):
        for s, (q0, upd, col) in enumerate(per_seq):
            st = s_ref[s, h]
            o = o + _dot(q0, st.astype(BF16))
            sout_ref[s, h] = col * st + upd
        outs.append(o)
    for h, o in enumerate(outs):
        o_ref[:, h * DV_A:(h + 1) * DV_A] = _gla_norm(
            o, gng_ref[:, h * DV_A:(h + 1) * DV_A]).astype(o_ref.dtype)


def _gla_sample(qk, v, glr, wgf, bgf, gng, mats, s0, s_all, layer, batch, seq):
    nseq = GLA_CHUNK // seq
    steps = batch // nseq
    row = lambda i: (i, 0)
    const = lambda i: (0, 0)
    lconst = lambda i: (layer, 0, 0)
    st = pl.BlockSpec((None, nseq, H_A, DK_A, DV_A), lambda i: (layer, i, 0, 0, 0))
    return pl.pallas_call(
        functools.partial(_gla_sample_kernel, seq_len=seq),
        grid=(steps,),
        in_specs=[pl.BlockSpec((GLA_CHUNK, 2 * GLA_QK), row),
                  pl.BlockSpec((GLA_CHUNK, GLA_V), row),
                  pl.BlockSpec((GLA_CHUNK, GLR_PAD), row),
                  pl.BlockSpec((None, GLR_PAD, GLA_QK), lconst),
                  pl.BlockSpec((None, 1, GLA_QK), lconst),
                  pl.BlockSpec((None, 1, GLA_V), lconst),
                  pl.BlockSpec(mats.shape, const), st, pl.BlockSpec(memory_space=pl.ANY)],
        out_specs=[pl.BlockSpec((GLA_CHUNK, GLA_V), lambda i: (i, 0)), st],
        out_shape=[jax.ShapeDtypeStruct((batch * seq, GLA_V), BF16),
                   jax.ShapeDtypeStruct(s_all.shape, F32)],
        input_output_aliases={8: 1},
        compiler_params=_cparams(("parallel",)),
        name="gla_sample",
    )(qk, v, glr, wgf, bgf, gng, mats, s0, s_all)


def _rope_tables(pos):
    half = ROT_DIM // 2
    inv = ROPE_THETA ** (-jnp.arange(half, dtype=F32) * 2.0 / ROT_DIM)
    ang = pos.astype(F32)[:, None] * inv[None, :]
    cos, sin = jnp.cos(ang), jnp.sin(ang)
    n = pos.shape[0]
    one = jnp.ones((n, HD_B - ROT_DIM), F32)
    zero = jnp.zeros((n, HD_B - ROT_DIM), F32)
    zh = jnp.zeros((n, half), F32)
    cos_t = jnp.concatenate([cos, cos, one], axis=1)
    sa_t = jnp.concatenate([-sin, zh, zero], axis=1)
    sb_t = jnp.concatenate([zh, sin, zero], axis=1)
    rep = LANES // HD_B
    return (jnp.tile(cos_t, (1, rep)), jnp.tile(sa_t, (1, rep)), jnp.tile(sb_t, (1, rep)))


def _rope(x, cos, sa, sb):
    outs = []
    for c in range(x.shape[1] // LANES):
        xb = x[:, c * LANES:(c + 1) * LANES]
        outs.append(xb * cos + pltpu.roll(xb, LANES - ROT_DIM // 2, 1) * sa
                    + pltpu.roll(xb, ROT_DIM // 2, 1) * sb)
    return outs[0] if len(outs) == 1 else jnp.concatenate(outs, axis=1)


def _swa_attend(seqs, sinks_ref, layer, o_ref, nq):
    half = LANES // 2
    lane = lax.broadcasted_iota(jnp.int32, (2 * WINDOW, LANES), 1)
    bias4 = {}
    for sq in seqs:
        if id(sq[3]) not in bias4:
            bias4[id(sq[3])] = jnp.concatenate([sq[3]] * 4, axis=0)
    rblk = lax.broadcasted_iota(jnp.int32, (4 * nq, 1), 0) >> int(math.log2(nq))
    sinks = []
    for g in range(KV_B):
        p = g % 2
        sink = jnp.zeros((4 * nq, 1), F32)
        for blk, head in enumerate((4 * g + p, 4 * g + 2 + p, 4 * g + 1 - p, 4 * g + 3 - p)):
            sink = jnp.where(rblk == blk, sinks_ref[layer, head], sink)
        sinks.append(sink)
    probs = [(q, k_all, v_all, bias4[id(bias)], g) for (q, k_all, v_all, bias) in seqs
             for g in range(KV_B)]

    q4s = []
    for q, _, _, _, g in probs:
        qp = jnp.concatenate([q[:, (2 * g) * LANES:(2 * g + 1) * LANES],
                              q[:, (2 * g + 1) * LANES:(2 * g + 2) * LANES]], axis=0)
        qp = qp * (HD_B ** -0.5)
        q4s.append(jnp.concatenate([qp, pltpu.roll(qp, half, 1)], axis=0).astype(BF16))
    ss = []
    for (_, k_all, _, b4, g), q4 in zip(probs, q4s):
        col, p = g // 2, g % 2
        in_half = (lane >= p * half) & (lane < (p + 1) * half)
        km = jnp.where(in_half, k_all[:, col * LANES:(col + 1) * LANES], 0.0).astype(BF16)
        ss.append(_dot_nt(q4, km) + b4)
    mxs = [jnp.maximum(jnp.max(s, axis=-1, keepdims=True), sinks[pr[4]])
           for pr, s in zip(probs, ss)]
    es = [jnp.exp(s - mx) for s, mx in zip(ss, mxs)]
    dens = [jnp.sum(e, axis=-1, keepdims=True) + jnp.exp(sinks[pr[4]] - mx)
            for pr, e, mx in zip(probs, es, mxs)]
    o4s = []
    for (_, _, v_all, _, g), e, den in zip(probs, es, dens):
        col, p = g // 2, g % 2
        in_half = (lane >= p * half) & (lane < (p + 1) * half)
        vm = jnp.where(in_half, v_all[:, col * LANES:(col + 1) * LANES], 0.0).astype(BF16)
        o4s.append(_dot((e / den).astype(BF16), vm))
    o2s = [o4[0:2 * nq] + pltpu.roll(o4[2 * nq:4 * nq], half, 1) for o4 in o4s]
    for g in range(KV_B):
        mine = [o2 for pr, o2 in zip(probs, o2s) if pr[4] == g]
        for t in range(2):
            slab = jnp.concatenate([o2[t * nq:(t + 1) * nq] for o2 in mine], axis=0)
            o_ref[:, (2 * g + t) * LANES:(2 * g + t + 1) * LANES] = slab.astype(o_ref.dtype)


def _swa_bias(nq, first_key):
    ri = lax.broadcasted_iota(jnp.int32, (nq, 2 * WINDOW), 0)
    ci = lax.broadcasted_iota(jnp.int32, (nq, 2 * WINDOW), 1)
    rel = ri - ci + WINDOW
    valid = (rel >= 0) & (rel < WINDOW) & (ci >= first_key)
    return jnp.where(valid, 0.0, -jnp.inf)


def _swa_tile(sq_ref, kv_ref, cos, sa, sb, first_tile, sinks_ref, layer, k_scr, v_scr, o_ref):
    k_rot = _rope(kv_ref[:, 0:SWA_KV], cos, sa, sb)
    v_new = kv_ref[:, SWA_KV:2 * SWA_KV]
    q = _rope(sq_ref[...].astype(F32), cos, sa, sb)
    k_all = jnp.concatenate([k_scr[...], k_rot], axis=0)
    v_all = jnp.concatenate([v_scr[...], v_new], axis=0)
    bias_first = _swa_bias(WINDOW, jnp.where(first_tile, WINDOW, 0))
    bias_rest = _swa_bias(WINDOW, 0)
    seqs = []
    for w in range(ROW_TILE // WINDOW):
        rows = slice(w * WINDOW, (w + 2) * WINDOW)
        seqs.append((q[w * WINDOW:(w + 1) * WINDOW], k_all[rows], v_all[rows],
                     bias_first if w == 0 else bias_rest))
    _swa_attend(seqs, sinks_ref, layer, o_ref, WINDOW)
    k_scr[...] = k_rot[ROW_TILE - WINDOW:]
    v_scr[...] = v_new[ROW_TILE - WINDOW:]


def _swa_sample_kernel(sinks_ref, sq_ref, kv_ref, cos_ref, sa_ref, sb_ref, ck_ref, cv_ref,
                       kall_in, vall_in, o_ref, kk_ref, vk_ref, *, layer, seq):
    del kall_in, vall_in
    cos, sa, sb = cos_ref[...], sa_ref[...], sb_ref[...]
    bias = _swa_bias(seq, 0)
    pad = jnp.zeros((WINDOW - seq, SWA_KV), F32)
    reps = SWA_SAMPLE_SEQS
    cos, sa, sb = (jnp.concatenate([t] * reps, axis=0) for t in (cos, sa, sb))
    k_rot = _rope(kv_ref[:, 0:SWA_KV], cos, sa, sb)
    q_rot = _rope(sq_ref[...].astype(F32), cos, sa, sb)
    seqs = []
    for i in range(SWA_SAMPLE_SEQS):
        rows = slice(i * seq, (i + 1) * seq)
        k_new, v_new = k_rot[rows], kv_ref[rows, SWA_KV:2 * SWA_KV]
        k_old, v_old = ck_ref[i], cv_ref[i]
        seqs.append((q_rot[rows], jnp.concatenate([k_old, k_new, pad], axis=0),
                     jnp.concatenate([v_old, v_new, pad], axis=0), bias))
        kk_ref[i] = jnp.concatenate([k_old[seq:], k_new], axis=0)
        vk_ref[i] = jnp.concatenate([v_old[seq:], v_new], axis=0)
    _swa_attend(seqs, sinks_ref, layer, o_ref, seq)


def _swa_sample(sinks, sq, kv, tabs, cache_k, cache_v, k_all, v_all, layer, batch, seq):
    nb = SWA_SAMPLE_SEQS
    rows = nb * seq
    row = lambda i: (i, 0)
    tab = lambda i: (0, 0)
    cache_in = pl.BlockSpec((None, nb, WINDOW, SWA_KV), lambda i: (layer, i, 0, 0))
    cache_out = cache_in
    any_ = pl.BlockSpec(memory_space=pl.ANY)
    return pl.pallas_call(
        functools.partial(_swa_sample_kernel, layer=layer, seq=seq),
        grid=(batch // nb,),
        in_specs=[pl.BlockSpec(memory_space=pltpu.SMEM),
                  pl.BlockSpec((rows, SWA_Q), row),
                  pl.BlockSpec((rows, 2 * SWA_KV), row),
                  pl.BlockSpec((seq, LANES), tab),
                  pl.BlockSpec((seq, LANES), tab),
                  pl.BlockSpec((seq, LANES), tab),
                  cache_in, cache_in, any_, any_],
        out_specs=[pl.BlockSpec((rows, SWA_Q), lambda i: (i, 0)), cache_out, cache_out],
        out_shape=[jax.ShapeDtypeStruct((batch * seq, SWA_Q), BF16),
                   jax.ShapeDtypeStruct(k_all.shape, F32),
                   jax.ShapeDtypeStruct(v_all.shape, F32)],
        input_output_aliases={8: 1, 9: 2},
        compiler_params=_cparams(("parallel",)),
        name="swa_sample",
    )(sinks, sq, kv, *tabs, cache_k, cache_v, k_all, v_all)


def _merge_tile(oa_ref, r_ref, ob_ref, gab_ref, x, wa_ref, wb_ref, wo_ref, bm_ref, g_ref, b_ref):
    r = r_ref[...].astype(F32)
    a = (oa_ref[...].astype(F32) * (r * jax.nn.sigmoid(r))).astype(BF16)
    ya = _dot(a, wa_ref[...])
    yb = _dot(ob_ref[...].astype(BF16), wb_ref[...])
    ga = jax.nn.sigmoid(gab_ref[:, 0:D_MODEL].astype(F32) + bm_ref[:, 0:D_MODEL])
    gb = jax.nn.sigmoid(gab_ref[:, D_MODEL:2 * D_MODEL].astype(F32) + bm_ref[:, D_MODEL:2 * D_MODEL])
    merged = (ga * ya + gb * yb).astype(BF16)
    y = ALPHA * x + _dot(merged, wo_ref[...])
    return _layer_norm_rows(y, g_ref[...], b_ref[...])


def _route_tile(y, whi_ref, wlo_ref, rb_ref, tri_ref, carry, route_ref, route_t_ref, counts_ref):
    rec = _route_rows(y, whi_ref, wlo_ref, rb_ref, tri_ref, carry)
    route_ref[...] = rec
    route_t_ref[...] = jnp.transpose(rec)[0:SUBLANES, :]
    counts_ref[...] = carry[...]


def _merge_kernel(oa_ref, r_ref, ob_ref, gab_ref, x_ref, wa_ref, wb_ref, wo_ref,
                  bm_ref, g_ref, b_ref, whi_ref, wlo_ref, rb_ref, tri_ref, cin_ref,
                  y_ref, pk_ref, route_ref, route_t_ref, counts_ref, carry):
    @pl.when(pl.program_id(0) == 0)
    def _():
        carry[...] = cin_ref[...]

    y = _merge_tile(oa_ref, r_ref, ob_ref, gab_ref, x_ref[...], wa_ref, wb_ref, wo_ref,
                    bm_ref, g_ref, b_ref)
    y_ref[...] = y
    pk_ref[...] = _pack_rows(y)
    _route_tile(y, whi_ref, wlo_ref, rb_ref, tri_ref, carry, route_ref, route_t_ref, counts_ref)


def _merge(oa, r, ob, gab, x, wa, wb, wo, bm, g, b, wr, br, tri, counts_in, layer):
    n = oa.shape[0]
    whi, wlo = wr
    row = lambda w: pl.BlockSpec((ROW_TILE, w), lambda i: (i, 0))
    const = lambda s: pl.BlockSpec(s, lambda i: (0, 0))
    lconst = lambda r_, c_: pl.BlockSpec((None, r_, c_), lambda i: (layer, 0, 0))
    return pl.pallas_call(
        _merge_kernel,
        grid=(n // ROW_TILE,),
        in_specs=[row(GLA_V), row(GLA_V), row(SWA_Q), row(2 * D_MODEL),
                  row(D_MODEL),
                  lconst(GLA_V, D_MODEL), lconst(SWA_Q, D_MODEL), lconst(D_MODEL, D_MODEL),
                  lconst(1, 2 * D_MODEL), lconst(1, D_MODEL), lconst(1, D_MODEL),
                  lconst(D_MODEL, ROUTE_LANES), lconst(D_MODEL, ROUTE_LANES), lconst(1, ROUTE_LANES),
                  const((ROW_TILE, ROW_TILE)), const((SUBLANES, ROUTE_LANES))],
        out_specs=[row(D_MODEL), row(PACKED), row(ROUTE_LANES),
                   pl.BlockSpec((SUBLANES, ROW_TILE), lambda i: (0, i)),
                   const((SUBLANES, ROUTE_LANES))],
        out_shape=[jax.ShapeDtypeStruct((n, D_MODEL), F32),
                   jax.ShapeDtypeStruct((n, PACKED), jnp.uint32),
                   jax.ShapeDtypeStruct((n, ROUTE_LANES), F32),
                   jax.ShapeDtypeStruct((SUBLANES, n), F32),
                   jax.ShapeDtypeStruct((SUBLANES, ROUTE_LANES), F32)],
        scratch_shapes=[pltpu.VMEM((SUBLANES, ROUTE_LANES), F32)],
        compiler_params=_cparams(("arbitrary",)),
        name="merge_route",
    )(oa, r, ob, gab, x, wa, wb, wo, bm, g, b, whi, wlo, br, tri, counts_in)


def _mixer_prompt_kernel(sinks_ref, x_ref, w_ref, wgf_ref, bgf_ref, gng_ref, m_ref,
                         cos_ref, sa_ref, sb_ref, wa_ref, wb_ref, wo_ref, bm_ref, g_ref, b_ref,
                         whi_ref, wlo_ref, rb_ref, tri_ref,
                         y_ref, pk_ref, route_ref, route_t_ref, counts_ref, sout_ref, kk_ref, vk_ref,
                         qk_s, v_s, r_s, sq_s, kv_s, gab_s, glr_s, oa_s, ob_s,
                         s_scr, k_scr, v_scr, carry, *, layer):
    seq_i, tile = pl.program_id(0), pl.program_id(1)

    @pl.when(jnp.logical_and(seq_i == 0, tile == 0))
    def _():
        carry[...] = jnp.zeros_like(carry)

    @pl.when(tile == 0)
    def _():
        s_scr[...] = jnp.zeros_like(s_scr)
        k_scr[...] = jnp.zeros_like(k_scr)
        v_scr[...] = jnp.zeros_like(v_scr)

    x = x_ref[...]
    xb = x.astype(BF16)
    proj = dict(zip(("qk", "v", "r", "sq", "kv", "gab", "glr"),
                    zip((qk_s, v_s, r_s, sq_s, kv_s, gab_s, glr_s), PROJ_WIDTHS,
                        (C_QK, C_V, C_R, C_SQ, C_KV, C_GAB, C_GLR))))

    def project(name, c0=0, c1=None):
        ref, wd, col = proj[name]
        c1 = wd if c1 is None else c1
        ref[:, c0:c1] = _dot(xb, w_ref[:, col + c0:col + c1]).astype(ref.dtype)

    for name in ("qk", "v", "glr"):
        project(name)
    later = iter([("sq",), ("kv",), ("r",), ("gab", 0, D_MODEL), ("gab", D_MODEL, 2 * D_MODEL)])

    def fill():
        piece = next(later, None)
        if piece is not None:
            project(*piece)

    _gla_tile(qk_s, v_s, glr_s, wgf_ref, bgf_ref, gng_ref, m_ref, s_scr, oa_s, fill)
    for _ in range(5):
        fill()
    _swa_tile(sq_s, kv_s, cos_ref[...], sa_ref[...], sb_ref[...], tile == 0, sinks_ref, layer,
              k_scr, v_scr, ob_s)
    y = _merge_tile(oa_s, r_s, ob_s, gab_s, x, wa_ref, wb_ref, wo_ref, bm_ref, g_ref, b_ref)
    y_ref[...] = y
    pk_ref[...] = _pack_rows(y)
    _route_tile(y, whi_ref, wlo_ref, rb_ref, tri_ref, carry, route_ref, route_t_ref, counts_ref)

    @pl.when(tile == pl.num_programs(1) - 1)
    def _():
        sout_ref[0] = s_scr[...]
        kk_ref[0] = k_scr[...]
        vk_ref[0] = v_scr[...]


def _mixer_prompt(sinks, x, w_proj, wgf, bgf, gng, mats, tabs, wa, wb, wo, bm, g, b, wr, br, tri,
                  layer, batch, seq):
    nt = seq // ROW_TILE
    n = batch * seq
    whi, wlo = wr
    row = lambda w: pl.BlockSpec((ROW_TILE, w), lambda s, t: (s * nt + t, 0))
    tab = pl.BlockSpec((ROW_TILE, LANES), lambda s, t: (t, 0))
    const = lambda shp: pl.BlockSpec(shp, lambda s, t: (0, 0))
    lconst = lambda r_, c_: pl.BlockSpec((None, r_, c_), lambda s, t: (layer, 0, 0),
                                         pipeline_mode=pl.Buffered(1))
    keep = lambda s, t: (s, 0, 0)
    act = lambda wd, dt: pltpu.VMEM((ROW_TILE, wd), dt)
    return pl.pallas_call(
        functools.partial(_mixer_prompt_kernel, layer=layer),
        grid=(batch, nt),
        in_specs=[pl.BlockSpec(memory_space=pltpu.SMEM), row(D_MODEL), lconst(D_MODEL, N_PROJ),
                  lconst(GLR_PAD, GLA_QK), lconst(1, GLA_QK), lconst(1, GLA_V), const(mats.shape),
                  tab, tab, tab,
                  lconst(GLA_V, D_MODEL), lconst(SWA_Q, D_MODEL), lconst(D_MODEL, D_MODEL),
                  lconst(1, 2 * D_MODEL), lconst(1, D_MODEL), lconst(1, D_MODEL),
                  lconst(D_MODEL, ROUTE_LANES), lconst(D_MODEL, ROUTE_LANES), lconst(1, ROUTE_LANES),
                  const((ROW_TILE, ROW_TILE))],
        out_specs=[row(D_MODEL), row(PACKED), row(ROUTE_LANES),
                   pl.BlockSpec((SUBLANES, ROW_TILE), lambda s, t: (0, s * nt + t)),
                   const((SUBLANES, ROUTE_LANES)),
                   pl.BlockSpec((1, H_A, DK_A, DV_A), lambda s, t: (s, 0, 0, 0)),
                   pl.BlockSpec((1, WINDOW, SWA_KV), keep), pl.BlockSpec((1, WINDOW, SWA_KV), keep)],
        out_shape=[jax.ShapeDtypeStruct((n, D_MODEL), F32),
                   jax.ShapeDtypeStruct((n, PACKED), jnp.uint32),
                   jax.ShapeDtypeStruct((n, ROUTE_LANES), F32),
                   jax.ShapeDtypeStruct((SUBLANES, n), F32),
                   jax.ShapeDtypeStruct((SUBLANES, ROUTE_LANES), F32),
                   jax.ShapeDtypeStruct((batch, H_A, DK_A, DV_A), F32),
                   jax.ShapeDtypeStruct((batch, WINDOW, SWA_KV), F32),
                   jax.ShapeDtypeStruct((batch, WINDOW, SWA_KV), F32)],
        scratch_shapes=[act(wd, dt) for wd, dt in zip(PROJ_WIDTHS, _proj_dtypes())]
        + [act(GLA_V, BF16), act(SWA_Q, BF16),
           pltpu.VMEM((H_A, DK_A, DV_A), F32), pltpu.VMEM((WINDOW, SWA_KV), F32),
           pltpu.VMEM((WINDOW, SWA_KV), F32), pltpu.VMEM((SUBLANES, ROUTE_LANES), F32)],
        compiler_params=_cparams(("arbitrary", "arbitrary")),
        name="mixer_prompt",
    )(sinks, x, w_proj, wgf, bgf, gng, mats, *tabs, wa, wb, wo, bm, g, b, whi, wlo, br, tri)


ROUTE_LANES = LANES
R_E0, R_E1, R_W0, R_W1, R_RANK0, R_RANK1 = 0, 1, 2, 3, 4, 5


def _route_rows(x, whi_ref, wlo_ref, b_ref, tri_ref, carry):
    xhi = x.astype(BF16)
    xlo = (x - xhi.astype(F32)).astype(BF16)
    whi = whi_ref[...]
    logits = _dot(xhi, whi) + _dot(xlo, whi) + _dot(xhi, wlo_ref[...]) + b_ref[...]
    t = logits.shape[0]
    lane = lax.broadcasted_iota(jnp.int32, (t, ROUTE_LANES), 1).astype(F32)
    neg = -jnp.inf
    big = float(ROUTE_LANES)

    def first_argmax(vals, mask):
        mv = jnp.where(mask, vals, neg)
        mx = jnp.max(mv, axis=-1, keepdims=True)
        idx = jnp.min(jnp.where(mask & (mv == mx), lane, big), axis=-1, keepdims=True)
        return mx, idx

    gmask = lane < N_GROUPS
    gmax, grp = first_argmax(logits, gmask)
    pg = 1.0 / jnp.sum(jnp.where(gmask, jnp.exp(logits - gmax), 0.0), axis=-1, keepdims=True)
    lo = N_GROUPS + grp * EXPERTS_PER_GROUP
    emask = (lane >= lo) & (lane < lo + EXPERTS_PER_GROUP)
    m0, i0 = first_argmax(logits, emask)
    m1, i1 = first_argmax(logits, emask & (lane != i0))
    r = jnp.exp(m1 - m0)
    w0 = pg / (1.0 + r)
    w1 = pg * r / (1.0 + r)
    e0 = i0 - N_GROUPS
    e1 = i1 - N_GROUPS

    oh0 = lane == e0
    oh1 = lane == e1
    both = jnp.where(oh0 | oh1, 1.0, 0.0)
    before = _dot(tri_ref[...], both.astype(BF16)) + carry[0:1, :]
    rank0 = jnp.sum(jnp.where(oh0, before, 0.0), axis=-1, keepdims=True)
    rank1 = jnp.sum(jnp.where(oh1, before, 0.0), axis=-1, keepdims=True)
    carry[...] = carry[...] + jnp.sum(both, axis=0, keepdims=True)

    rec = jnp.zeros((t, ROUTE_LANES), F32)
    for ln, val in ((R_E0, e0), (R_E1, e1), (R_W0, w0), (R_W1, w1),
                    (R_RANK0, rank0), (R_RANK1, rank1)):
        rec = jnp.where(lane == ln, val, rec)
    return rec


def _dest_kernel(ps_ref, rt_ref, d_ref):
    rt = rt_ref[...]
    start = jnp.zeros_like(rt)
    for e in range(N_EXPERTS):
        start = jnp.where(rt == float(e), ps_ref[e].astype(F32), start)
    ranks = pltpu.roll(rt, SUBLANES - R_RANK0, 0)
    d_ref[...] = (start + ranks).astype(jnp.int32)


def _dest(pstart, route_t):
    return pl.pallas_call(
        _dest_kernel,
        in_specs=[pl.BlockSpec(memory_space=pltpu.SMEM), pl.BlockSpec(memory_space=pltpu.VMEM)],
        out_specs=pl.BlockSpec(memory_space=pltpu.VMEM),
        out_shape=jax.ShapeDtypeStruct(route_t.shape, jnp.int32),
        name="moe_dest",
    )(pstart, route_t)


def _dispatch_kernel(d0_ref, d1_ref, xp_ref, xs_ref, rows_in_hbm, rows_hbm, xbuf, sems, *,
                     prompt_tiles):
    del rows_in_hbm
    i = pl.program_id(0)
    last = pl.num_programs(0) - 1
    slot = lax.rem(i, 2)

    def drain(s):
        for _ in range(2):
            pltpu.make_async_copy(xbuf.at[s], rows_hbm.at[pl.ds(0, ROW_TILE)], sems.at[s]).wait()

    @pl.when(i >= 2)
    def _():
        drain(slot)

    xbuf[slot] = jnp.where(i < prompt_tiles, xp_ref[...], xs_ref[...])
    for t in range(ROW_TILE):
        src = xbuf.at[slot, pl.ds(t, 1)]
        pltpu.make_async_copy(src, rows_hbm.at[pl.ds(d0_ref[0, 0, t], 1)], sems.at[slot]).start()
        pltpu.make_async_copy(src, rows_hbm.at[pl.ds(d1_ref[0, 0, t], 1)],
                              sems.at[slot]).start(priority=1)

    @pl.when(i == last)
    def _():
        drain(slot)

    @pl.when(jnp.logical_and(i == last, last >= 1))
    def _():
        drain(1 - slot)


def _dest_specs():
    idx = pl.BlockSpec((1, 1, ROW_TILE), lambda i: (i, 0, 0), memory_space=pltpu.SMEM)
    return [idx, idx]


def _two_group_specs(x_p, width):
    pt = x_p.shape[0] // ROW_TILE
    return (pt, pl.BlockSpec((ROW_TILE, width), lambda i: (jnp.minimum(i, pt - 1), 0)),
            pl.BlockSpec((ROW_TILE, width), lambda i: (jnp.maximum(i - pt, 0), 0)))


def _dispatch(dests, x_p, x_s, rows_zero):
    n = x_p.shape[0] + x_s.shape[0]
    any_ = pl.BlockSpec(memory_space=pl.ANY)
    pt, pspec, sspec = _two_group_specs(x_p, PACKED)
    return pl.pallas_call(
        functools.partial(_dispatch_kernel, prompt_tiles=pt),
        grid=(n // ROW_TILE,),
        in_specs=_dest_specs() + [pspec, sspec, any_],
        out_specs=any_,
        out_shape=jax.ShapeDtypeStruct(rows_zero.shape, rows_zero.dtype),
        scratch_shapes=[pltpu.VMEM((2, ROW_TILE, PACKED), jnp.uint32), pltpu.SemaphoreType.DMA((2,))],
        input_output_aliases={4: 0},
        compiler_params=_cparams(("arbitrary",)),
        name="moe_dispatch",
    )(*dests, x_p, x_s, rows_zero)


def _expert_kernel(be_ref, nu_ref, rows_ref, wg_ref, wu_ref, wd_ref, y_ref, wg_b, wu_b, wd_b):
    i = pl.program_id(0)
    used = i < nu_ref[0]
    new_expert = jnp.logical_or(i == 0, be_ref[i] != be_ref[jnp.maximum(i - 1, 0)])

    @pl.when(jnp.logical_and(used, new_expert))
    def _():
        wg_b[...] = wg_ref[0].astype(BF16)
        wu_b[...] = wu_ref[0].astype(BF16)
        wd_b[...] = wd_ref[0].astype(BF16)

    @pl.when(used)
    def _():
        xb = _unpack_rows(rows_ref[...]).astype(BF16)
        g = _dot(xb, wg_b[...])
        u = _dot(xb, wu_b[...])
        h = (g * jax.nn.sigmoid(g) * u).astype(BF16)
        y_ref[...] = _pack_rows(_dot(h, wd_b[...]))

    @pl.when(jnp.logical_not(used))
    def _():
        y_ref[...] = jnp.zeros_like(y_ref)


def _experts(blk_e, n_used, rows, wg, wu, wd, layer):
    nb = rows.shape[0] // MOE_BLOCK
    rblk = lambda i, be, nu: (jnp.maximum(jnp.minimum(i, nu[0] - 1), 0), 0)
    wsel = lambda i, be, nu: (layer, be[i], 0, 0)
    gs = pltpu.PrefetchScalarGridSpec(
        num_scalar_prefetch=2,
        grid=(nb,),
        in_specs=[pl.BlockSpec((MOE_BLOCK, PACKED), rblk),
                  pl.BlockSpec((None, 1, D_MODEL, EXPERT_FF), wsel),
                  pl.BlockSpec((None, 1, D_MODEL, EXPERT_FF), wsel),
                  pl.BlockSpec((None, 1, EXPERT_FF, D_MODEL), wsel)],
        out_specs=pl.BlockSpec((MOE_BLOCK, PACKED), lambda i, be, nu: (i, 0)),
        scratch_shapes=[pltpu.VMEM((D_MODEL, EXPERT_FF), BF16), pltpu.VMEM((D_MODEL, EXPERT_FF), BF16),
                        pltpu.VMEM((EXPERT_FF, D_MODEL), BF16)])
    return pl.pallas_call(
        _expert_kernel,
        grid_spec=gs,
        out_shape=jax.ShapeDtypeStruct(rows.shape, rows.dtype),
        compiler_params=_cparams(("arbitrary",)),
        name="moe_experts",
    )(blk_e, n_used, rows, wg, wu, wd)


def _combine_kernel(d0_ref, d1_ref, d0n_ref, d1n_ref, y_hbm, route_ref, xp_ref, xs_ref, g_ref, b_ref,
                    op_ref, os_ref, buf, sems, *, prompt_tiles):
    i = pl.program_id(0)
    last = pl.num_programs(0) - 1
    slot = lax.rem(i, 2)

    def gather(da_ref, db_ref, s):
        for t in range(ROW_TILE):
            pltpu.make_async_copy(y_hbm.at[pl.ds(da_ref[0, 0, t], 1)], buf.at[s, 0, pl.ds(t, 1)],
                                  sems.at[s]).start()
            pltpu.make_async_copy(y_hbm.at[pl.ds(db_ref[0, 0, t], 1)], buf.at[s, 1, pl.ds(t, 1)],
                                  sems.at[s]).start(priority=1)

    def drain(s):
        for k in range(2):
            pltpu.make_async_copy(y_hbm.at[pl.ds(0, ROW_TILE)], buf.at[s, k], sems.at[s]).wait()

    @pl.when(i == 0)
    def _():
        gather(d0_ref, d1_ref, 0)

    gather(d0n_ref, d1n_ref, 1 - slot)
    drain(slot)
    w0 = route_ref[:, R_W0:R_W0 + 1]
    w1 = route_ref[:, R_W1:R_W1 + 1]
    moe = w0 * _unpack_rows(buf[slot, 0]) + w1 * _unpack_rows(buf[slot, 1])
    is_prompt = i < prompt_tiles
    x = jnp.where(is_prompt, xp_ref[...], xs_ref[...])
    out = _layer_norm_rows(ALPHA * x + moe, g_ref[...], b_ref[...])

    @pl.when(is_prompt)
    def _():
        op_ref[...] = out

    @pl.when(jnp.logical_not(is_prompt))
    def _():
        os_ref[...] = out

    @pl.when(i == last)
    def _():
        drain(1 - slot)


def _combine(dests, y, route, x_p, x_s, g, b, layer):
    n = x_p.shape[0] + x_s.shape[0]
    tiles = n // ROW_TILE
    row = lambda w: pl.BlockSpec((ROW_TILE, w), lambda i: (i, 0))
    lconst = lambda r_, c_: pl.BlockSpec((None, r_, c_), lambda i: (layer, 0, 0))
    nxt = pl.BlockSpec((1, 1, ROW_TILE), lambda i: (jnp.minimum(i + 1, tiles - 1), 0, 0),
                       memory_space=pltpu.SMEM)
    pt, pspec, sspec = _two_group_specs(x_p, D_MODEL)
    return pl.pallas_call(
        functools.partial(_combine_kernel, prompt_tiles=pt),
        grid=(tiles,),
        in_specs=_dest_specs() + [nxt, nxt, pl.BlockSpec(memory_space=pl.ANY), row(ROUTE_LANES),
                                  pspec, sspec, lconst(1, D_MODEL), lconst(1, D_MODEL)],
        out_specs=[pspec, sspec],
        out_shape=[jax.ShapeDtypeStruct(x_p.shape, F32), jax.ShapeDtypeStruct(x_s.shape, F32)],
        scratch_shapes=[pltpu.VMEM((2, 2, ROW_TILE, PACKED), jnp.uint32),
                        pltpu.SemaphoreType.DMA((2,))],
        compiler_params=_cparams(("arbitrary",)),
        name="moe_combine",
    )(*dests, *dests, y, route, x_p, x_s, g, b)


def _moe_rows_buffer(n):
    nb = (2 * n + N_EXPERTS * (MOE_BLOCK - 1)) // MOE_BLOCK
    return jnp.zeros((nb * MOE_BLOCK, PACKED), jnp.uint32)


def _moe_layer(x_p, x_s, pk_p, pk_s, route, route_t, counts, rows_buf, wg, wu, wd, g, b, layer):
    n = x_p.shape[0] + x_s.shape[0]
    cnt = counts[0, :N_EXPERTS].astype(jnp.int32)
    padded = (cnt + MOE_BLOCK - 1) // MOE_BLOCK * MOE_BLOCK
    pend = jnp.cumsum(padded)
    pstart = pend - padded
    nb = (2 * n + N_EXPERTS * (MOE_BLOCK - 1)) // MOE_BLOCK
    blk_start = jnp.arange(nb, dtype=jnp.int32) * MOE_BLOCK
    blk_e = jnp.minimum(jnp.sum((pend[None, :] <= blk_start[:, None]).astype(jnp.int32), axis=1),
                        N_EXPERTS - 1)
    n_used = pend[-1:] // MOE_BLOCK
    dest = _dest(pstart, route_t).reshape(SUBLANES, n // ROW_TILE, 1, ROW_TILE)
    dests = (dest[R_E0], dest[R_E1])
    rows = _dispatch(dests, pk_p, pk_s, rows_buf)
    y = _experts(blk_e, n_used, rows, wg, wu, wd, layer)
    x_p, x_s = _combine(dests, y, route, x_p, x_s, g, b, layer)
    return x_p, x_s, rows


def _prep_weights(w_in, w_gf2, w_router_group, b_router_group, w_router_expert, b_router_expert):
    zpad = jnp.zeros((DEPTH, D_MODEL, GLR_PAD - GATE_RANK), BF16)
    glr_end = W_IN_GLR + GATE_RANK
    w_proj = jnp.concatenate([w_in[:, :, :W_IN_GLR].astype(BF16), w_in[:, :, glr_end:].astype(BF16),
                              w_in[:, :, W_IN_GLR:glr_end].astype(BF16), zpad], axis=2)
    wgf = jnp.concatenate([w_gf2, jnp.zeros((DEPTH, GLR_PAD - GATE_RANK, GLA_QK), F32)],
                          axis=1).astype(BF16)
    rpad = ROUTE_LANES - N_GROUPS - N_EXPERTS
    wr = jnp.concatenate([w_router_group, w_router_expert,
                          jnp.zeros((DEPTH, D_MODEL, rpad), F32)], axis=2)
    br = jnp.concatenate([b_router_group, b_router_expert, jnp.zeros((DEPTH, rpad), F32)], axis=1)
    wr_hi = wr.astype(BF16)
    wr_lo = (wr - wr_hi.astype(F32)).astype(BF16)
    return w_proj, wgf, (wr_hi, wr_lo), br[:, None, :]


def kernel(x_prompt, x_sample, state_gla, cache_swa_k, cache_swa_v, w_in, w_gf2, b_gf, gla_norm_g, sinks,
           b_merge, w_branch_a, w_branch_b, w_out, ln1_g, ln1_b, w_router_group, b_router_group,
           w_router_expert, b_router_expert, w_gate_e, w_up_e, w_down_e, ln2_g, ln2_b):
    bp, tp, _ = x_prompt.shape
    bs, ts, _ = x_sample.shape
    n_p, n_s = bp * tp, bs * ts
    w_proj, wgf, wr, br = _prep_weights(w_in, w_gf2, w_router_group, b_router_group,
                                        w_router_expert, b_router_expert)
    wa, wb, wo = w_branch_a.astype(BF16), w_branch_b.astype(BF16), w_out.astype(BF16)
    wg, wu, wd = w_gate_e, w_up_e, w_down_e
    mats_p = jnp.asarray(_gla_matrices(GLA_CHUNK), BF16)
    mats_s = jnp.asarray(_gla_matrices(ts), BF16)
    tabs_p = _rope_tables(jnp.arange(tp, dtype=jnp.int32))
    tabs_s = _rope_tables(PAST_LEN + jnp.arange(ts, dtype=jnp.int32))
    tri = jnp.asarray(np.tril(np.ones((ROW_TILE, ROW_TILE), np.float32), -1), BF16)
    cache_k = cache_swa_k.reshape(DEPTH, bs, WINDOW, SWA_KV)
    cache_v = cache_swa_v.reshape(DEPTH, bs, WINDOW, SWA_KV)

    bgf, gng = b_gf[:, None, :], gla_norm_g.reshape(DEPTH, 1, GLA_V)
    bm, g1, b1 = b_merge[:, None, :], ln1_g[:, None, :], ln1_b[:, None, :]
    g2, b2 = ln2_g[:, None, :], ln2_b[:, None, :]

    x_p, x_s = x_prompt.reshape(n_p, D_MODEL), x_sample.reshape(n_s, D_MODEL)
    st_s = jnp.zeros((DEPTH, bs, H_A, DK_A, DV_A), F32)
    kk_s = jnp.zeros((DEPTH, bs, WINDOW, SWA_KV), F32)
    vk_s = jnp.zeros((DEPTH, bs, WINDOW, SWA_KV), F32)
    rows_buf = _moe_rows_buffer(n_p + n_s)
    outs = [[] for _ in range(3)]
    for l in range(DEPTH):
        y_p, pk_p, route_p, rt_p, counts_p, st_p, kk_p, vk_p = _mixer_prompt(
            sinks, x_p, w_proj, wgf, bgf, gng, mats_p, tabs_p, wa, wb, wo, bm, g1, b1, wr, br, tri,
            l, bp, tp)
        qk, v, r, sq, kv, gab, glr = _inproj(x_s, w_proj, l)
        oa_s, st_s = _gla_sample(qk, v, glr, wgf, bgf, gng, mats_s, state_gla, st_s, l, bs, ts)
        ob_s, kk_s, vk_s = _swa_sample(sinks, sq, kv, tabs_s, cache_k, cache_v, kk_s, vk_s,
                                       l, bs, ts)
        y_s, pk_s, route_s, rt_s, counts = _merge(oa_s, r, ob_s, gab, x_s, wa, wb, wo, bm, g1, b1,
                                                  wr, br, tri, counts_p, l)
        route = jnp.concatenate([route_p, route_s], axis=0)
        route_t = jnp.concatenate([rt_p, rt_s], axis=1)
        x_p, x_s, rows_buf = _moe_layer(y_p, y_s, pk_p, pk_s, route, route_t, counts, rows_buf,
                                        wg, wu, wd, g2, b2, l)
        for lst, val in zip(outs, (st_p, kk_p, vk_p)):
            lst.append(val)

    st_p, kk_p, vk_p = [jnp.stack(o) for o in outs]
    return (x_p.reshape(bp, tp, D_MODEL), x_s.reshape(bs, ts, D_MODEL),
            st_p.astype(state_gla.dtype),
            kk_p.reshape(DEPTH, bp, WINDOW, KV_B, HD_B), vk_p.reshape(DEPTH, bp, WINDOW, KV_B, HD_B),
            st_s.astype(state_gla.dtype),
            kk_s.reshape(DEPTH, bs, WINDOW, KV_B, HD_B), vk_s.reshape(DEPTH, bs, WINDOW, KV_B, HD_B))
```

```python
import functools
import math

import numpy as np
import jax
import jax.numpy as jnp
from jax import lax
from jax.experimental import pallas as pl
from jax.experimental.pallas import tpu as pltpu

F32 = jnp.float32
BF16 = jnp.bfloat16

D_MODEL = 1024
DEPTH = 4
PAST_LEN = 8192
H_A, DK_A, DV_A = 4, 128, 256
GLA_QK, GLA_V = H_A * DK_A, H_A * DV_A
GATE_RANK = 16
GATE_TAU = 16.0
GLA_CHUNK = 64
HD_B, H_B, KV_B = 64, 16, 4
G_B = H_B // KV_B
SWA_Q, SWA_KV = H_B * HD_B, KV_B * HD_B
WINDOW = 128
ROT_DIM = HD_B // 4
ROPE_THETA = 500000.0
N_GROUPS, EXPERTS_PER_GROUP = 4, 8
N_EXPERTS = N_GROUPS * EXPERTS_PER_GROUP
EXPERT_FF = D_MODEL // 4
ALPHA = (2 * DEPTH) ** 0.25
LN_EPS = 1e-5
NORM_EPS = 1e-6

LANES = 128
SUBLANES = 8
VMEM_LIMIT_BYTES = 56 * 1024 * 1024

ROW_TILE = 256
MOE_BLOCK = 512
MOE_TILE = 512
GLR_PAD = LANES
GLA_STEP_CHUNKS = ROW_TILE // GLA_CHUNK
SWA_SAMPLE_SEQS = 8

C_QK, C_V, C_R = 0, GLA_QK * 2, GLA_QK * 2 + GLA_V
W_IN_GLR = C_R + GLA_V
C_SQ = W_IN_GLR
C_KV, C_GAB = C_SQ + SWA_Q, C_SQ + SWA_Q + 2 * SWA_KV
C_GLR = C_GAB + 2 * D_MODEL
N_PROJ = C_GLR + GLR_PAD


def _cparams(sem):
    return pltpu.CompilerParams(dimension_semantics=sem, vmem_limit_bytes=VMEM_LIMIT_BYTES)


def _dot(a, b):
    return jnp.dot(a, b, preferred_element_type=F32)


def _dot_nt(a, b):
    return lax.dot_general(a, b, (((1,), (1,)), ((), ())), preferred_element_type=F32)


def _dot_tn(a, b):
    return lax.dot_general(a, b, (((0,), (0,)), ((), ())), preferred_element_type=F32)


def _layer_norm_rows(y, g, b):
    mu = jnp.mean(y, axis=-1, keepdims=True)
    d = y - mu
    var = jnp.mean(d * d, axis=-1, keepdims=True)
    return d * lax.rsqrt(var + LN_EPS) * g + b


PACKED = D_MODEL // 2


def _pack_rows(x):
    lo = pltpu.bitcast(x[:, :PACKED].astype(jnp.bfloat16).astype(F32), jnp.uint32)
    hi = pltpu.bitcast(x[:, PACKED:].astype(jnp.bfloat16).astype(F32), jnp.uint32)
    return hi | (lo >> 16)


def _unpack_rows(w):
    lo = pltpu.bitcast(w << 16, F32)
    hi = pltpu.bitcast(w & jnp.uint32(0xFFFF0000), F32)
    return jnp.concatenate([lo, hi], axis=1)


PROJ_WIDTHS = (1024, 1024, 1024, 1024, 512, 2048, GLR_PAD)
def _proj_dtypes():
    return (F32, BF16, BF16, BF16, F32, BF16, F32)


def _inproj_tile(x, w_ref, out_refs):
    xb = x.astype(BF16)
    col = 0
    for ref, wd in zip(out_refs, PROJ_WIDTHS):
        ref[...] = _dot(xb, w_ref[:, col:col + wd]).astype(ref.dtype)
        col += wd


def _inproj_kernel(x_ref, w_ref, *out_refs):
    _inproj_tile(x_ref[...], w_ref, out_refs)


def _inproj(x, w, layer):
    n = x.shape[0]
    widths, dtypes = PROJ_WIDTHS, _proj_dtypes()
    return pl.pallas_call(
        _inproj_kernel,
        grid=(n // ROW_TILE,),
        in_specs=[pl.BlockSpec((ROW_TILE, D_MODEL), lambda i: (i, 0)),
                  pl.BlockSpec((None, D_MODEL, N_PROJ), lambda i: (layer, 0, 0))],
        out_specs=[pl.BlockSpec((ROW_TILE, wd), lambda i: (i, 0)) for wd in widths],
        out_shape=[jax.ShapeDtypeStruct((n, wd), dt) for wd, dt in zip(widths, dtypes)],
        compiler_params=_cparams(("parallel",)),
        name="inproj",
    )(x, w)


def _gla_matrices(seq_len):
    c = GLA_CHUNK
    p = np.arange(c)[:, None]
    r = np.arange(c)[None, :]
    same = (p // seq_len) == (r // seq_len)
    return np.concatenate([same & (r <= p), same & (r > p)], axis=0).astype(np.float32)


def _log_sigmoid(z):
    return -(jnp.maximum(-z, 0.0) + jnp.log(1.0 + jnp.exp(-jnp.abs(z))))


def _level_exponents(b, logf, n, rowi):
    c = b.shape[0]
    if n == 1:
        return jnp.where((rowi & 1) == 1, logf, 0.0)
    pieces = []
    if n >= SUBLANES:
        for j in range(c // (2 * n)):
            mid = j * 2 * n + n - 1
            pieces.append(jnp.broadcast_to(b[mid:mid + 1, :], (2 * n, DK_A)))
    else:
        sub = rowi[0:SUBLANES]
        for g in range(c // SUBLANES):
            bg = b[g * SUBLANES:(g + 1) * SUBLANES]
            piece = None
            for j in range(SUBLANES // (2 * n)):
                mid = j * 2 * n + n - 1
                bc = jnp.broadcast_to(bg[mid:mid + 1, :], (SUBLANES, DK_A))
                piece = bc if piece is None else jnp.where(sub >= j * 2 * n, bc, piece)
            pieces.append(piece)
    bmid = jnp.concatenate(pieces, axis=0)
    second = ((rowi >> int(math.log2(n))) & 1) == 1
    return jnp.where(second, b - bmid, bmid - b)


def _no_fill():
    pass


def _gla_local(probs, m_ref, seq_len, fill=_no_fill):
    c = GLA_CHUNK
    nseq = c // seq_len
    m = m_ref[...]
    rowi = lax.broadcasted_iota(jnp.int32, (c, DK_A), 0)
    ri = lax.broadcasted_iota(jnp.int32, (c, c), 0)
    ci = lax.broadcasted_iota(jnp.int32, (c, c), 1)

    zs = [_dot(glr_b, wgf) + bgf for (_, _, _, glr_b, wgf, bgf) in probs]
    fill()
    logfs = [_log_sigmoid(z) * (1.0 / GATE_TAU) for z in zs]
    sums = []
    for logf in logfs:
        hi = logf.astype(BF16)
        r1 = logf - hi.astype(F32)
        mid = r1.astype(BF16)
        lo = (r1 - mid.astype(F32)).astype(BF16)
        sums.append(_dot(m, hi) + _dot(m, mid) + _dot(m, lo))
    fill()

    qs = [p[0] * (DK_A ** -0.5) for p in probs]
    ks = [p[1] for p in probs]
    atts = [jnp.where(ri == ci, jnp.sum(q * k, axis=-1, keepdims=True), 0.0)
            for q, k in zip(qs, ks)]
    n = seq_len // 2
    while n >= 1:
        sh = int(math.log2(n))
        second = ((rowi >> sh) & 1) == 1
        for i, (q, k) in enumerate(zip(qs, ks)):
            e = jnp.exp(_level_exponents(sums[i][0:c], logfs[i], n, rowi))
            ql = jnp.where(second, q * e, 0.0).astype(BF16)
            kl = jnp.where(second, 0.0, k * e).astype(BF16)
            a = _dot_nt(ql, kl)
            if 2 * n < c:
                a = jnp.where((ri >> (sh + 1)) == (ci >> (sh + 1)), a, 0.0)
            atts[i] = atts[i] + a
        fill()
        n //= 2

    vbs = [p[2].astype(BF16) for p in probs]
    o_intra = [_dot(att.astype(BF16), vb) for att, vb in zip(atts, vbs)]
    q0s = [q * jnp.exp(sm[0:c]) for q, sm in zip(qs, sums)]
    kds = [k * jnp.exp(sm[c:2 * c]) for k, sm in zip(ks, sums)]
    out = []
    for i in range(len(probs)):
        per_seq = []
        for s in range(nseq):
            if nseq == 1:
                q0, kd = q0s[i].astype(BF16), kds[i].astype(BF16)
            else:
                in_seq = (rowi >> int(math.log2(seq_len))) == s
                q0 = jnp.where(in_seq, q0s[i], 0.0).astype(BF16)
                kd = jnp.where(in_seq, kds[i], 0.0).astype(BF16)
            last = s * seq_len + seq_len - 1
            ebl = jnp.exp(sums[i][last:last + 1, :])
            col = jnp.transpose(jnp.broadcast_to(ebl, (SUBLANES, DK_A)))[:, 0:1]
            per_seq.append((q0, _dot_tn(kd, vbs[i]), col))
        out.append((o_intra[i], per_seq))
    return out


def _gla_norm(o, gng):
    return o * lax.rsqrt(jnp.mean(o * o, axis=-1, keepdims=True) + NORM_EPS) * gng


def _gla_tile(qk_ref, v_ref, glr_ref, wgf_ref, bgf_ref, gng_ref, m_ref, s_scr, o_ref, fill=_no_fill):
    probs, where = [], []
    for c in range(GLA_STEP_CHUNKS):
        rows = slice(c * GLA_CHUNK, (c + 1) * GLA_CHUNK)
        glr_b = glr_ref[rows, :].astype(BF16)
        for h in range(H_A):
            probs.append((qk_ref[rows, h * DK_A:(h + 1) * DK_A],
                          qk_ref[rows, GLA_QK + h * DK_A:GLA_QK + (h + 1) * DK_A],
                          v_ref[rows, h * DV_A:(h + 1) * DV_A], glr_b,
                          wgf_ref[:, h * DK_A:(h + 1) * DK_A], bgf_ref[:, h * DK_A:(h + 1) * DK_A]))
            where.append((rows, h))
    local = _gla_local(probs, m_ref, GLA_CHUNK, fill)
    outs = []
    for (rows, h), (o_intra, ((q0, upd, col),)) in zip(where, local):
        st = s_scr[h]
        outs.append(o_intra + _dot(q0, st.astype(BF16)))
        s_scr[h] = col * st + upd
    for (rows, h), o in zip(where, outs):
        o_ref[rows, h * DV_A:(h + 1) * DV_A] = _gla_norm(
            o, gng_ref[:, h * DV_A:(h + 1) * DV_A]).astype(o_ref.dtype)


def _gla_sample_kernel(qk_ref, v_ref, glr_ref, wgf_ref, bgf_ref, gng_ref, m_ref, s_ref, all_in,
                       o_ref, sout_ref, *, seq_len):
    del all_in
    nseq = GLA_CHUNK // seq_len
    glr_b = glr_ref[...].astype(BF16)
    probs = [(qk_ref[:, h * DK_A:(h + 1) * DK_A],
              qk_ref[:, GLA_QK + h * DK_A:GLA_QK + (h + 1) * DK_A],
              v_ref[:, h * DV_A:(h + 1) * DV_A], glr_b,
              wgf_ref[:, h * DK_A:(h + 1) * DK_A], bgf_ref[:, h * DK_A:(h + 1) * DK_A])
             for h in range(H_A)]
    local = _gla_local(probs, m_ref, seq_len)
    outs = []
    for h, (o, per_seq) in enumerate(local):
        for s, (q0, upd, col) in enumerate(per_seq):
            st = s_ref[s, h]
            o = o + _dot(q0, st.astype(BF16))
            sout_ref[s, h] = col * st + upd
        outs.append(o)
    for h, o in enumerate(outs):
        o_ref[:, h * DV_A:(h + 1) * DV_A] = _gla_norm(
            o, gng_ref[:, h * DV_A:(h + 1) * DV_A]).astype(o_ref.dtype)


def _gla_sample(qk, v, glr, wgf, bgf, gng, mats, s0, s_all, layer, batch, seq):
    nseq = GLA_CHUNK // seq
    steps = batch // nseq
    row = lambda i: (i, 0)
    const = lambda i: (0, 0)
    lconst = lambda i: (layer, 0, 0)
    st = pl.BlockSpec((None, nseq, H_A, DK_A, DV_A), lambda i: (layer, i, 0, 0, 0))
    return pl.pallas_call(
        functools.partial(_gla_sample_kernel, seq_len=seq),
        grid=(steps,),
        in_specs=[pl.BlockSpec((GLA_CHUNK, 2 * GLA_QK), row),
                  pl.BlockSpec((GLA_CHUNK, GLA_V), row),
                  pl.BlockSpec((GLA_CHUNK, GLR_PAD), row),
                  pl.BlockSpec((None, GLR_PAD, GLA_QK), lconst),
                  pl.BlockSpec((None, 1, GLA_QK), lconst),
                  pl.BlockSpec((None, 1, GLA_V), lconst),
                  pl.BlockSpec(mats.shape, const), st, pl.BlockSpec(memory_space=pl.ANY)],
        out_specs=[pl.BlockSpec((GLA_CHUNK, GLA_V), lambda i: (i, 0)), st],
        out_shape=[jax.ShapeDtypeStruct((batch * seq, GLA_V), BF16),
                   jax.ShapeDtypeStruct(s_all.shape, F32)],
        input_output_aliases={8: 1},
        compiler_params=_cparams(("parallel",)),
        name="gla_sample",
    )(qk, v, glr, wgf, bgf, gng, mats, s0, s_all)


def _rope_tables(pos):
    half = ROT_DIM // 2
    inv = ROPE_THETA ** (-jnp.arange(half, dtype=F32) * 2.0 / ROT_DIM)
    ang = pos.astype(F32)[:, None] * inv[None, :]
    cos, sin = jnp.cos(ang), jnp.sin(ang)
    n = pos.shape[0]
    one = jnp.ones((n, HD_B - ROT_DIM), F32)
    zero = jnp.zeros((n, HD_B - ROT_DIM), F32)
    zh = jnp.zeros((n, half), F32)
    cos_t = jnp.concatenate([cos, cos, one], axis=1)
    sa_t = jnp.concatenate([-sin, zh, zero], axis=1)
    sb_t = jnp.concatenate([zh, sin, zero], axis=1)
    rep = LANES // HD_B
    return (jnp.tile(cos_t, (1, rep)), jnp.tile(sa_t, (1, rep)), jnp.tile(sb_t, (1, rep)))


def _rope(x, cos, sa, sb):
    outs = []
    for c in range(x.shape[1] // LANES):
        xb = x[:, c * LANES:(c + 1) * LANES]
        outs.append(xb * cos + pltpu.roll(xb, LANES - ROT_DIM // 2, 1) * sa
                    + pltpu.roll(xb, ROT_DIM // 2, 1) * sb)
    return outs[0] if len(outs) == 1 else jnp.concatenate(outs, axis=1)


def _swa_attend(seqs, sinks_ref, layer, o_ref, nq):
    half = LANES // 2
    lane = lax.broadcasted_iota(jnp.int32, (2 * WINDOW, LANES), 1)
    bias4 = {}
    for sq in seqs:
        if id(sq[3]) not in bias4:
            bias4[id(sq[3])] = jnp.concatenate([sq[3]] * 4, axis=0)
    rblk = lax.broadcasted_iota(jnp.int32, (4 * nq, 1), 0) >> int(math.log2(nq))
    sinks = []
    for g in range(KV_B):
        p = g % 2
        sink = jnp.zeros((4 * nq, 1), F32)
        for blk, head in enumerate((4 * g + p, 4 * g + 2 + p, 4 * g + 1 - p, 4 * g + 3 - p)):
            sink = jnp.where(rblk == blk, sinks_ref[layer, head], sink)
        sinks.append(sink)
    probs = [(q, k_all, v_all, bias4[id(bias)], g) for (q, k_all, v_all, bias) in seqs
             for g in range(KV_B)]

    q4s = []
    for q, _, _, _, g in probs:
        qp = jnp.concatenate([q[:, (2 * g) * LANES:(2 * g + 1) * LANES],
                              q[:, (2 * g + 1) * LANES:(2 * g + 2) * LANES]], axis=0)
        qp = qp * (HD_B ** -0.5)
        q4s.append(jnp.concatenate([qp, pltpu.roll(qp, half, 1)], axis=0).astype(BF16))
    ss = []
    for (_, k_all, _, b4, g), q4 in zip(probs, q4s):
        col, p = g // 2, g % 2
        in_half = (lane >= p * half) & (lane < (p + 1) * half)
        km = jnp.where(in_half, k_all[:, col * LANES:(col + 1) * LANES], 0.0).astype(BF16)
        ss.append(_dot_nt(q4, km) + b4)
    mxs = [jnp.maximum(jnp.max(s, axis=-1, keepdims=True), sinks[pr[4]])
           for pr, s in zip(probs, ss)]
    es = [jnp.exp(s - mx) for s, mx in zip(ss, mxs)]
    dens = [jnp.sum(e, axis=-1, keepdims=True) + jnp.exp(sinks[pr[4]] - mx)
            for pr, e, mx in zip(probs, es, mxs)]
    o4s = []
    for (_, _, v_all, _, g), e, den in zip(probs, es, dens):
        col, p = g // 2, g % 2
        in_half = (lane >= p * half) & (lane < (p + 1) * half)
        vm = jnp.where(in_half, v_all[:, col * LANES:(col + 1) * LANES], 0.0).astype(BF16)
        o4s.append(_dot((e / den).astype(BF16), vm))
    o2s = [o4[0:2 * nq] + pltpu.roll(o4[2 * nq:4 * nq], half, 1) for o4 in o4s]
    for g in range(KV_B):
        mine = [o2 for pr, o2 in zip(probs, o2s) if pr[4] == g]
        for t in range(2):
            slab = jnp.concatenate([o2[t * nq:(t + 1) * nq] for o2 in mine], axis=0)
            o_ref[:, (2 * g + t) * LANES:(2 * g + t + 1) * LANES] = slab.astype(o_ref.dtype)


def _swa_bias(nq, first_key):
    ri = lax.broadcasted_iota(jnp.int32, (nq, 2 * WINDOW), 0)
    ci = lax.broadcasted_iota(jnp.int32, (nq, 2 * WINDOW), 1)
    rel = ri - ci + WINDOW
    valid = (rel >= 0) & (rel < WINDOW) & (ci >= first_key)
    return jnp.where(valid, 0.0, -jnp.inf)


def _swa_tile(sq_ref, kv_ref, cos, sa, sb, first_tile, sinks_ref, layer, k_scr, v_scr, o_ref):
    k_rot = _rope(kv_ref[:, 0:SWA_KV], cos, sa, sb)
    v_new = kv_ref[:, SWA_KV:2 * SWA_KV]
    q = _rope(sq_ref[...].astype(F32), cos, sa, sb)
    k_all = jnp.concatenate([k_scr[...], k_rot], axis=0)
    v_all = jnp.concatenate([v_scr[...], v_new], axis=0)
    bias_first = _swa_bias(WINDOW, jnp.where(first_tile, WINDOW, 0))
    bias_rest = _swa_bias(WINDOW, 0)
    seqs = []
    for w in range(ROW_TILE // WINDOW):
        rows = slice(w * WINDOW, (w + 2) * WINDOW)
        seqs.append((q[w * WINDOW:(w + 1) * WINDOW], k_all[rows], v_all[rows],
                     bias_first if w == 0 else bias_rest))
    _swa_attend(seqs, sinks_ref, layer, o_ref, WINDOW)
    k_scr[...] = k_rot[ROW_TILE - WINDOW:]
    v_scr[...] = v_new[ROW_TILE - WINDOW:]


def _swa_sample_kernel(sinks_ref, sq_ref, kv_ref, cos_ref, sa_ref, sb_ref, ck_ref, cv_ref,
                       kall_in, vall_in, o_ref, kk_ref, vk_ref, *, layer, seq):
    del kall_in, vall_in
    cos, sa, sb = cos_ref[...], sa_ref[...], sb_ref[...]
    bias = _swa_bias(seq, 0)
    pad = jnp.zeros((WINDOW - seq, SWA_KV), F32)
    reps = SWA_SAMPLE_SEQS
    cos, sa, sb = (jnp.concatenate([t] * reps, axis=0) for t in (cos, sa, sb))
    k_rot = _rope(kv_ref[:, 0:SWA_KV], cos, sa, sb)
    q_rot = _rope(sq_ref[...].astype(F32), cos, sa, sb)
    seqs = []
    for i in range(SWA_SAMPLE_SEQS):
        rows = slice(i * seq, (i + 1) * seq)
        k_new, v_new = k_rot[rows], kv_ref[rows, SWA_KV:2 * SWA_KV]
        k_old, v_old = ck_ref[i], cv_ref[i]
        seqs.append((q_rot[rows], jnp.concatenate([k_old, k_new, pad], axis=0),
                     jnp.concatenate([v_old, v_new, pad], axis=0), bias))
        kk_ref[i] = jnp.concatenate([k_old[seq:], k_new], axis=0)
        vk_ref[i] = jnp.concatenate([v_old[seq:], v_new], axis=0)
    _swa_attend(seqs, sinks_ref, layer, o_ref, seq)


def _swa_sample(sinks, sq, kv, tabs, cache_k, cache_v, k_all, v_all, layer, batch, seq):
    nb = SWA_SAMPLE_SEQS
    rows = nb * seq
    row = lambda i: (i, 0)
    tab = lambda i: (0, 0)
    cache_in = pl.BlockSpec((None, nb, WINDOW, SWA_KV), lambda i: (layer, i, 0, 0))
    cache_out = cache_in
    any_ = pl.BlockSpec(memory_space=pl.ANY)
    return pl.pallas_call(
        functools.partial(_swa_sample_kernel, layer=layer, seq=seq),
        grid=(batch // nb,),
        in_specs=[pl.BlockSpec(memory_space=pltpu.SMEM),
                  pl.BlockSpec((rows, SWA_Q), row),
                  pl.BlockSpec((rows, 2 * SWA_KV), row),
                  pl.BlockSpec((seq, LANES), tab),
                  pl.BlockSpec((seq, LANES), tab),
                  pl.BlockSpec((seq, LANES), tab),
                  cache_in, cache_in, any_, any_],
        out_specs=[pl.BlockSpec((rows, SWA_Q), lambda i: (i, 0)), cache_out, cache_out],
        out_shape=[jax.ShapeDtypeStruct((batch * seq, SWA_Q), BF16),
                   jax.ShapeDtypeStruct(k_all.shape, F32),
                   jax.ShapeDtypeStruct(v_all.shape, F32)],
        input_output_aliases={8: 1, 9: 2},
        compiler_params=_cparams(("parallel",)),
        name="swa_sample",
    )(sinks, sq, kv, *tabs, cache_k, cache_v, k_all, v_all)


def _merge_tile(oa_ref, r_ref, ob_ref, gab_ref, x, wa_ref, wb_ref, wo_ref, bm_ref, g_ref, b_ref):
    r = r_ref[...].astype(F32)
    a = (oa_ref[...].astype(F32) * (r * jax.nn.sigmoid(r))).astype(BF16)
    ya = _dot(a, wa_ref[...])
    yb = _dot(ob_ref[...].astype(BF16), wb_ref[...])
    ga = jax.nn.sigmoid(gab_ref[:, 0:D_MODEL].astype(F32) + bm_ref[:, 0:D_MODEL])
    gb = jax.nn.sigmoid(gab_ref[:, D_MODEL:2 * D_MODEL].astype(F32) + bm_ref[:, D_MODEL:2 * D_MODEL])
    merged = (ga * ya + gb * yb).astype(BF16)
    y = ALPHA * x + _dot(merged, wo_ref[...])
    return _layer_norm_rows(y, g_ref[...], b_ref[...])


def _route_tile(y, whi_ref, wlo_ref, rb_ref, tri_ref, carry, route_ref, route_t_ref, counts_ref):
    rec = _route_rows(y, whi_ref, wlo_ref, rb_ref, tri_ref, carry)
    route_ref[...] = rec
    route_t_ref[...] = jnp.transpose(rec)[0:SUBLANES, :]
    counts_ref[...] = carry[...]


def _merge_kernel(oa_ref, r_ref, ob_ref, gab_ref, x_ref, wa_ref, wb_ref, wo_ref,
                  bm_ref, g_ref, b_ref, whi_ref, wlo_ref, rb_ref, tri_ref, cin_ref,
                  y_ref, pk_ref, route_ref, route_t_ref, counts_ref, carry):
    @pl.when(pl.program_id(0) == 0)
    def _():
        carry[...] = cin_ref[...]

    y = _merge_tile(oa_ref, r_ref, ob_ref, gab_ref, x_ref[...], wa_ref, wb_ref, wo_ref,
                    bm_ref, g_ref, b_ref)
    y_ref[...] = y
    pk_ref[...] = _pack_rows(y)
    _route_tile(y, whi_ref, wlo_ref, rb_ref, tri_ref, carry, route_ref, route_t_ref, counts_ref)


def _merge(oa, r, ob, gab, x, wa, wb, wo, bm, g, b, wr, br, tri, counts_in, layer):
    n = oa.shape[0]
    whi, wlo = wr
    row = lambda w: pl.BlockSpec((ROW_TILE, w), lambda i: (i, 0))
    const = lambda s: pl.BlockSpec(s, lambda i: (0, 0))
    lconst = lambda r_, c_: pl.BlockSpec((None, r_, c_), lambda i: (layer, 0, 0))
    return pl.pallas_call(
        _merge_kernel,
        grid=(n // ROW_TILE,),
        in_specs=[row(GLA_V), row(GLA_V), row(SWA_Q), row(2 * D_MODEL),
                  row(D_MODEL),
                  lconst(GLA_V, D_MODEL), lconst(SWA_Q, D_MODEL), lconst(D_MODEL, D_MODEL),
                  lconst(1, 2 * D_MODEL), lconst(1, D_MODEL), lconst(1, D_MODEL),
                  lconst(D_MODEL, ROUTE_LANES), lconst(D_MODEL, ROUTE_LANES), lconst(1, ROUTE_LANES),
                  const((ROW_TILE, ROW_TILE)), const((SUBLANES, ROUTE_LANES))],
        out_specs=[row(D_MODEL), row(PACKED), row(ROUTE_LANES),
                   pl.BlockSpec((SUBLANES, ROW_TILE), lambda i: (0, i)),
                   const((SUBLANES, ROUTE_LANES))],
        out_shape=[jax.ShapeDtypeStruct((n, D_MODEL), F32),
                   jax.ShapeDtypeStruct((n, PACKED), jnp.uint32),
                   jax.ShapeDtypeStruct((n, ROUTE_LANES), F32),
                   jax.ShapeDtypeStruct((SUBLANES, n), F32),
                   jax.ShapeDtypeStruct((SUBLANES, ROUTE_LANES), F32)],
        scratch_shapes=[pltpu.VMEM((SUBLANES, ROUTE_LANES), F32)],
        compiler_params=_cparams(("arbitrary",)),
        name="merge_route",
    )(oa, r, ob, gab, x, wa, wb, wo, bm, g, b, whi, wlo, br, tri, counts_in)


def _mixer_prompt_kernel(sinks_ref, x_ref, w_ref, wgf_ref, bgf_ref, gng_ref, m_ref,
                         cos_ref, sa_ref, sb_ref, wa_ref, wb_ref, wo_ref, bm_ref, g_ref, b_ref,
                         whi_ref, wlo_ref, rb_ref, tri_ref,
                         y_ref, pk_ref, route_ref, route_t_ref, counts_ref, sout_ref, kk_ref, vk_ref,
                         qk_s, v_s, r_s, sq_s, kv_s, gab_s, glr_s, oa_s, ob_s,
                         s_scr, k_scr, v_scr, carry, *, layer):
    seq_i, tile = pl.program_id(0), pl.program_id(1)

    @pl.when(jnp.logical_and(seq_i == 0, tile == 0))
    def _():
        carry[...] = jnp.zeros_like(carry)

    @pl.when(tile == 0)
    def _():
        s_scr[...] = jnp.zeros_like(s_scr)
        k_scr[...] = jnp.zeros_like(k_scr)
        v_scr[...] = jnp.zeros_like(v_scr)

    x = x_ref[...]
    xb = x.astype(BF16)
    proj = dict(zip(("qk", "v", "r", "sq", "kv", "gab", "glr"),
                    zip((qk_s, v_s, r_s, sq_s, kv_s, gab_s, glr_s), PROJ_WIDTHS,
                        (C_QK, C_V, C_R, C_SQ, C_KV, C_GAB, C_GLR))))

    def project(name, c0=0, c1=None):
        ref, wd, col = proj[name]
        c1 = wd if c1 is None else c1
        ref[:, c0:c1] = _dot(xb, w_ref[:, col + c0:col + c1]).astype(ref.dtype)

    for name in ("qk", "v", "glr"):
        project(name)
    later = iter([("sq",), ("kv",), ("r",), ("gab", 0, D_MODEL), ("gab", D_MODEL, 2 * D_MODEL)])

    def fill():
        piece = next(later, None)
        if piece is not None:
            project(*piece)

    _gla_tile(qk_s, v_s, glr_s, wgf_ref, bgf_ref, gng_ref, m_ref, s_scr, oa_s, fill)
    for _ in range(5):
        fill()
    _swa_tile(sq_s, kv_s, cos_ref[...], sa_ref[...], sb_ref[...], tile == 0, sinks_ref, layer,
              k_scr, v_scr, ob_s)
    y = _merge_tile(oa_s, r_s, ob_s, gab_s, x, wa_ref, wb_ref, wo_ref, bm_ref, g_ref, b_ref)
    y_ref[...] = y
    pk_ref[...] = _pack_rows(y)
    _route_tile(y, whi_ref, wlo_ref, rb_ref, tri_ref, carry, route_ref, route_t_ref, counts_ref)

    @pl.when(tile == pl.num_programs(1) - 1)
    def _():
        sout_ref[0] = s_scr[...]
        kk_ref[0] = k_scr[...]
        vk_ref[0] = v_scr[...]


def _mixer_prompt(sinks, x, w_proj, wgf, bgf, gng, mats, tabs, wa, wb, wo, bm, g, b, wr, br, tri,
                  layer, batch, seq):
    nt = seq // ROW_TILE
    n = batch * seq
    whi, wlo = wr
    row = lambda w: pl.BlockSpec((ROW_TILE, w), lambda s, t: (s * nt + t, 0))
    tab = pl.BlockSpec((ROW_TILE, LANES), lambda s, t: (t, 0))
    const = lambda shp: pl.BlockSpec(shp, lambda s, t: (0, 0))
    lconst = lambda r_, c_: pl.BlockSpec((None, r_, c_), lambda s, t: (layer, 0, 0),
                                         pipeline_mode=pl.Buffered(1))
    keep = lambda s, t: (s, 0, 0)
    act = lambda wd, dt: pltpu.VMEM((ROW_TILE, wd), dt)
    return pl.pallas_call(
        functools.partial(_mixer_prompt_kernel, layer=layer),
        grid=(batch, nt),
        in_specs=[pl.BlockSpec(memory_space=pltpu.SMEM), row(D_MODEL), lconst(D_MODEL, N_PROJ),
                  lconst(GLR_PAD, GLA_QK), lconst(1, GLA_QK), lconst(1, GLA_V), const(mats.shape),
                  tab, tab, tab,
                  lconst(GLA_V, D_MODEL), lconst(SWA_Q, D_MODEL), lconst(D_MODEL, D_MODEL),
                  lconst(1, 2 * D_MODEL), lconst(1, D_MODEL), lconst(1, D_MODEL),
                  lconst(D_MODEL, ROUTE_LANES), lconst(D_MODEL, ROUTE_LANES), lconst(1, ROUTE_LANES),
                  const((ROW_TILE, ROW_TILE))],
        out_specs=[row(D_MODEL), row(PACKED), row(ROUTE_LANES),
                   pl.BlockSpec((SUBLANES, ROW_TILE), lambda s, t: (0, s * nt + t)),
                   const((SUBLANES, ROUTE_LANES)),
                   pl.BlockSpec((1, H_A, DK_A, DV_A), lambda s, t: (s, 0, 0, 0)),
                   pl.BlockSpec((1, WINDOW, SWA_KV), keep), pl.BlockSpec((1, WINDOW, SWA_KV), keep)],
        out_shape=[jax.ShapeDtypeStruct((n, D_MODEL), F32),
                   jax.ShapeDtypeStruct((n, PACKED), jnp.uint32),
                   jax.ShapeDtypeStruct((n, ROUTE_LANES), F32),
                   jax.ShapeDtypeStruct((SUBLANES, n), F32),
                   jax.ShapeDtypeStruct((SUBLANES, ROUTE_LANES), F32),
                   jax.ShapeDtypeStruct((batch, H_A, DK_A, DV_A), F32),
                   jax.ShapeDtypeStruct((batch, WINDOW, SWA_KV), F32),
                   jax.ShapeDtypeStruct((batch, WINDOW, SWA_KV), F32)],
        scratch_shapes=[act(wd, dt) for wd, dt in zip(PROJ_WIDTHS, _proj_dtypes())]
        + [act(GLA_V, BF16), act(SWA_Q, BF16),
           pltpu.VMEM((H_A, DK_A, DV_A), F32), pltpu.VMEM((WINDOW, SWA_KV), F32),
           pltpu.VMEM((WINDOW, SWA_KV), F32), pltpu.VMEM((SUBLANES, ROUTE_LANES), F32)],
        compiler_params=_cparams(("arbitrary", "arbitrary")),
        name="mixer_prompt",
    )(sinks, x, w_proj, wgf, bgf, gng, mats, *tabs, wa, wb, wo, bm, g, b, whi, wlo, br, tri)


ROUTE_LANES = LANES
R_E0, R_E1, R_W0, R_W1, R_RANK0, R_RANK1 = 0, 1, 2, 3, 4, 5


def _route_rows(x, whi_ref, wlo_ref, b_ref, tri_ref, carry):
    xhi = x.astype(BF16)
    xlo = (x - xhi.astype(F32)).astype(BF16)
    whi = whi_ref[...]
    logits = _dot(xhi, whi) + _dot(xlo, whi) + _dot(xhi, wlo_ref[...]) + b_ref[...]
    t = logits.shape[0]
    lane = lax.broadcasted_iota(jnp.int32, (t, ROUTE_LANES), 1).astype(F32)
    neg = -jnp.inf
    big = float(ROUTE_LANES)

    def first_argmax(vals, mask):
        mv = jnp.where(mask, vals, neg)
        mx = jnp.max(mv, axis=-1, keepdims=True)
        idx = jnp.min(jnp.where(mask & (mv == mx), lane, big), axis=-1, keepdims=True)
        return mx, idx

    gmask = lane < N_GROUPS
    gmax, grp = first_argmax(logits, gmask)
    pg = 1.0 / jnp.sum(jnp.where(gmask, jnp.exp(logits - gmax), 0.0), axis=-1, keepdims=True)
    lo = N_GROUPS + grp * EXPERTS_PER_GROUP
    emask = (lane >= lo) & (lane < lo + EXPERTS_PER_GROUP)
    m0, i0 = first_argmax(logits, emask)
    m1, i1 = first_argmax(logits, emask & (lane != i0))
    r = jnp.exp(m1 - m0)
    w0 = pg / (1.0 + r)
    w1 = pg * r / (1.0 + r)
    e0 = i0 - N_GROUPS
    e1 = i1 - N_GROUPS

    oh0 = lane == e0
    oh1 = lane == e1
    both = jnp.where(oh0 | oh1, 1.0, 0.0)
    before = _dot(tri_ref[...], both.astype(BF16)) + carry[0:1, :]
    rank0 = jnp.sum(jnp.where(oh0, before, 0.0), axis=-1, keepdims=True)
    rank1 = jnp.sum(jnp.where(oh1, before, 0.0), axis=-1, keepdims=True)
    carry[...] = carry[...] + jnp.sum(both, axis=0, keepdims=True)

    rec = jnp.zeros((t, ROUTE_LANES), F32)
    for ln, val in ((R_E0, e0), (R_E1, e1), (R_W0, w0), (R_W1, w1),
                    (R_RANK0, rank0), (R_RANK1, rank1)):
        rec = jnp.where(lane == ln, val, rec)
    return rec


def _dest_kernel(ps_ref, rt_ref, d_ref):
    rt = rt_ref[...]
    start = jnp.zeros_like(rt)
    for e in range(N_EXPERTS):
        start = jnp.where(rt == float(e), ps_ref[e].astype(F32), start)
    ranks = pltpu.roll(rt, SUBLANES - R_RANK0, 0)
    d_ref[...] = (start + ranks).astype(jnp.int32)


def _dest(pstart, route_t):
    return pl.pallas_call(
        _dest_kernel,
        in_specs=[pl.BlockSpec(memory_space=pltpu.SMEM), pl.BlockSpec(memory_space=pltpu.VMEM)],
        out_specs=pl.BlockSpec(memory_space=pltpu.VMEM),
        out_shape=jax.ShapeDtypeStruct(route_t.shape, jnp.int32),
        name="moe_dest",
    )(pstart, route_t)


def _dispatch_kernel(d0_ref, d1_ref, xp_ref, xs_ref, rows_in_hbm, rows_hbm, xbuf, sems, *,
                     prompt_tiles):
    del rows_in_hbm
    i = pl.program_id(0)
    last = pl.num_programs(0) - 1
    slot = lax.rem(i, 2)

    def drain(s):
        for _ in range(2):
            pltpu.make_async_copy(xbuf.at[s], rows_hbm.at[pl.ds(0, MOE_TILE)], sems.at[s]).wait()

    @pl.when(i >= 2)
    def _():
        drain(slot)

    xbuf[slot] = jnp.where(i < prompt_tiles, xp_ref[...], xs_ref[...])
    for t in range(MOE_TILE):
        src = xbuf.at[slot, pl.ds(t, 1)]
        pltpu.make_async_copy(src, rows_hbm.at[pl.ds(d0_ref[0, 0, t], 1)], sems.at[slot]).start()
        pltpu.make_async_copy(src, rows_hbm.at[pl.ds(d1_ref[0, 0, t], 1)],
                              sems.at[slot]).start(priority=1)

    @pl.when(i == last)
    def _():
        drain(slot)

    @pl.when(jnp.logical_and(i == last, last >= 1))
    def _():
        drain(1 - slot)


def _dest_specs():
    idx = pl.BlockSpec((1, 1, MOE_TILE), lambda i: (i, 0, 0), memory_space=pltpu.SMEM)
    return [idx, idx]


def _two_group_specs(x_p, width):
    pt = x_p.shape[0] // MOE_TILE
    return (pt, pl.BlockSpec((MOE_TILE, width), lambda i: (jnp.minimum(i, pt - 1), 0)),
            pl.BlockSpec((MOE_TILE, width), lambda i: (jnp.maximum(i - pt, 0), 0)))


def _dispatch(dests, x_p, x_s, rows_zero):
    n = x_p.shape[0] + x_s.shape[0]
    any_ = pl.BlockSpec(memory_space=pl.ANY)
    pt, pspec, sspec = _two_group_specs(x_p, PACKED)
    return pl.pallas_call(
        functools.partial(_dispatch_kernel, prompt_tiles=pt),
        grid=(n // MOE_TILE,),
        in_specs=_dest_specs() + [pspec, sspec, any_],
        out_specs=any_,
        out_shape=jax.ShapeDtypeStruct(rows_zero.shape, rows_zero.dtype),
        scratch_shapes=[pltpu.VMEM((2, MOE_TILE, PACKED), jnp.uint32), pltpu.SemaphoreType.DMA((2,))],
        input_output_aliases={4: 0},
        compiler_params=_cparams(("arbitrary",)),
        name="moe_dispatch",
    )(*dests, x_p, x_s, rows_zero)


def _expert_kernel(be_ref, nu_ref, rows_ref, wg_ref, wu_ref, wd_ref, y_ref, wg_b, wu_b, wd_b):
    i = pl.program_id(0)
    used = i < nu_ref[0]
    new_expert = jnp.logical_or(i == 0, be_ref[i] != be_ref[jnp.maximum(i - 1, 0)])

    @pl.when(jnp.logical_and(used, new_expert))
    def _():
        wg_b[...] = wg_ref[0].astype(BF16)
        wu_b[...] = wu_ref[0].astype(BF16)
        wd_b[...] = wd_ref[0].astype(BF16)

    @pl.when(used)
    def _():
        xb = _unpack_rows(rows_ref[...]).astype(BF16)
        g = _dot(xb, wg_b[...])
        u = _dot(xb, wu_b[...])
        h = (g * jax.nn.sigmoid(g) * u).astype(BF16)
        y_ref[...] = _pack_rows(_dot(h, wd_b[...]))

    @pl.when(jnp.logical_not(used))
    def _():
        y_ref[...] = jnp.zeros_like(y_ref)


def _experts(blk_e, n_used, rows, wg, wu, wd, layer):
    nb = rows.shape[0] // MOE_BLOCK
    rblk = lambda i, be, nu: (jnp.maximum(jnp.minimum(i, nu[0] - 1), 0), 0)
    wsel = lambda i, be, nu: (layer, be[i], 0, 0)
    gs = pltpu.PrefetchScalarGridSpec(
        num_scalar_prefetch=2,
        grid=(nb,),
        in_specs=[pl.BlockSpec((MOE_BLOCK, PACKED), rblk),
                  pl.BlockSpec((None, 1, D_MODEL, EXPERT_FF), wsel),
                  pl.BlockSpec((None, 1, D_MODEL, EXPERT_FF), wsel),
                  pl.BlockSpec((None, 1, EXPERT_FF, D_MODEL), wsel)],
        out_specs=pl.BlockSpec((MOE_BLOCK, PACKED), lambda i, be, nu: (i, 0)),
        scratch_shapes=[pltpu.VMEM((D_MODEL, EXPERT_FF), BF16), pltpu.VMEM((D_MODEL, EXPERT_FF), BF16),
                        pltpu.VMEM((EXPERT_FF, D_MODEL), BF16)])
    return pl.pallas_call(
        _expert_kernel,
        grid_spec=gs,
        out_shape=jax.ShapeDtypeStruct(rows.shape, rows.dtype),
        compiler_params=_cparams(("arbitrary",)),
        name="moe_experts",
    )(blk_e, n_used, rows, wg, wu, wd)


def _combine_kernel(d0_ref, d1_ref, d0n_ref, d1n_ref, y_hbm, route_ref, xp_ref, xs_ref, g_ref, b_ref,
                    op_ref, os_ref, buf, sems, *, prompt_tiles):
    i = pl.program_id(0)
    last = pl.num_programs(0) - 1
    slot = lax.rem(i, 2)

    def gather(da_ref, db_ref, s):
        for t in range(MOE_TILE):
            pltpu.make_async_copy(y_hbm.at[pl.ds(da_ref[0, 0, t], 1)], buf.at[s, 0, pl.ds(t, 1)],
                                  sems.at[s]).start()
            pltpu.make_async_copy(y_hbm.at[pl.ds(db_ref[0, 0, t], 1)], buf.at[s, 1, pl.ds(t, 1)],
                                  sems.at[s]).start(priority=1)

    def drain(s):
        for k in range(2):
            pltpu.make_async_copy(y_hbm.at[pl.ds(0, MOE_TILE)], buf.at[s, k], sems.at[s]).wait()

    @pl.when(i == 0)
    def _():
        gather(d0_ref, d1_ref, 0)

    gather(d0n_ref, d1n_ref, 1 - slot)
    drain(slot)
    w0 = route_ref[:, R_W0:R_W0 + 1]
    w1 = route_ref[:, R_W1:R_W1 + 1]
    moe = w0 * _unpack_rows(buf[slot, 0]) + w1 * _unpack_rows(buf[slot, 1])
    is_prompt = i < prompt_tiles
    x = jnp.where(is_prompt, xp_ref[...], xs_ref[...])
    out = _layer_norm_rows(ALPHA * x + moe, g_ref[...], b_ref[...])

    @pl.when(is_prompt)
    def _():
        op_ref[...] = out

    @pl.when(jnp.logical_not(is_prompt))
    def _():
        os_ref[...] = out

    @pl.when(i == last)
    def _():
        drain(1 - slot)


def _combine(dests, y, route, x_p, x_s, g, b, layer):
    n = x_p.shape[0] + x_s.shape[0]
    tiles = n // MOE_TILE
    row = lambda w: pl.BlockSpec((MOE_TILE, w), lambda i: (i, 0))
    lconst = lambda r_, c_: pl.BlockSpec((None, r_, c_), lambda i: (layer, 0, 0))
    nxt = pl.BlockSpec((1, 1, MOE_TILE), lambda i: (jnp.minimum(i + 1, tiles - 1), 0, 0),
                       memory_space=pltpu.SMEM)
    pt, pspec, sspec = _two_group_specs(x_p, D_MODEL)
    return pl.pallas_call(
        functools.partial(_combine_kernel, prompt_tiles=pt),
        grid=(tiles,),
        in_specs=_dest_specs() + [nxt, nxt, pl.BlockSpec(memory_space=pl.ANY), row(ROUTE_LANES),
                                  pspec, sspec, lconst(1, D_MODEL), lconst(1, D_MODEL)],
        out_specs=[pspec, sspec],
        out_shape=[jax.ShapeDtypeStruct(x_p.shape, F32), jax.ShapeDtypeStruct(x_s.shape, F32)],
        scratch_shapes=[pltpu.VMEM((2, 2, MOE_TILE, PACKED), jnp.uint32),
                        pltpu.SemaphoreType.DMA((2,))],
        compiler_params=_cparams(("arbitrary",)),
        name="moe_combine",
    )(*dests, *dests, y, route, x_p, x_s, g, b)


def _moe_rows_buffer(n):
    nb = (2 * n + N_EXPERTS * (MOE_BLOCK - 1)) // MOE_BLOCK
    return jnp.zeros((nb * MOE_BLOCK, PACKED), jnp.uint32)


def _moe_layer(x_p, x_s, pk_p, pk_s, route, route_t, counts, rows_buf, wg, wu, wd, g, b, layer):
    n = x_p.shape[0] + x_s.shape[0]
    cnt = counts[0, :N_EXPERTS].astype(jnp.int32)
    padded = (cnt + MOE_BLOCK - 1) // MOE_BLOCK * MOE_BLOCK
    pend = jnp.cumsum(padded)
    pstart = pend - padded
    nb = (2 * n + N_EXPERTS * (MOE_BLOCK - 1)) // MOE_BLOCK
    blk_start = jnp.arange(nb, dtype=jnp.int32) * MOE_BLOCK
    blk_e = jnp.minimum(jnp.sum((pend[None, :] <= blk_start[:, None]).astype(jnp.int32), axis=1),
                        N_EXPERTS - 1)
    n_used = pend[-1:] // MOE_BLOCK
    dest = _dest(pstart, route_t).reshape(SUBLANES, n // MOE_TILE, 1, MOE_TILE)
    dests = (dest[R_E0], dest[R_E1])
    rows = _dispatch(dests, pk_p, pk_s, rows_buf)
    y = _experts(blk_e, n_used, rows, wg, wu, wd, layer)
    x_p, x_s = _combine(dests, y, route, x_p, x_s, g, b, layer)
    return x_p, x_s, rows


def _prep_weights(w_in, w_gf2, w_router_group, b_router_group, w_router_expert, b_router_expert):
    zpad = jnp.zeros((DEPTH, D_MODEL, GLR_PAD - GATE_RANK), BF16)
    glr_end = W_IN_GLR + GATE_RANK
    w_proj = jnp.concatenate([w_in[:, :, :W_IN_GLR].astype(BF16), w_in[:, :, glr_end:].astype(BF16),
                              w_in[:, :, W_IN_GLR:glr_end].astype(BF16), zpad], axis=2)
    wgf = jnp.concatenate([w_gf2, jnp.zeros((DEPTH, GLR_PAD - GATE_RANK, GLA_QK), F32)],
                          axis=1).astype(BF16)
    rpad = ROUTE_LANES - N_GROUPS - N_EXPERTS
    wr = jnp.concatenate([w_router_group, w_router_expert,
                          jnp.zeros((DEPTH, D_MODEL, rpad), F32)], axis=2)
    br = jnp.concatenate([b_router_group, b_router_expert, jnp.zeros((DEPTH, rpad), F32)], axis=1)
    wr_hi = wr.astype(BF16)
    wr_lo = (wr - wr_hi.astype(F32)).astype(BF16)
    return w_proj, wgf, (wr_hi, wr_lo), br[:, None, :]


def kernel(x_prompt, x_sample, state_gla, cache_swa_k, cache_swa_v, w_in, w_gf2, b_gf, gla_norm_g, sinks,
           b_merge, w_branch_a, w_branch_b, w_out, ln1_g, ln1_b, w_router_group, b_router_group,
           w_router_expert, b_router_expert, w_gate_e, w_up_e, w_down_e, ln2_g, ln2_b):
    bp, tp, _ = x_prompt.shape
    bs, ts, _ = x_sample.shape
    n_p, n_s = bp * tp, bs * ts
    w_proj, wgf, wr, br = _prep_weights(w_in, w_gf2, w_router_group, b_router_group,
                                        w_router_expert, b_router_expert)
    wa, wb, wo = w_branch_a.astype(BF16), w_branch_b.astype(BF16), w_out.astype(BF16)
    wg, wu, wd = w_gate_e, w_up_e, w_down_e
    mats_p = jnp.asarray(_gla_matrices(GLA_CHUNK), BF16)
    mats_s = jnp.asarray(_gla_matrices(ts), BF16)
    tabs_p = _rope_tables(jnp.arange(tp, dtype=jnp.int32))
    tabs_s = _rope_tables(PAST_LEN + jnp.arange(ts, dtype=jnp.int32))
    tri = jnp.asarray(np.tril(np.ones((ROW_TILE, ROW_TILE), np.float32), -1), BF16)
    cache_k = cache_swa_k.reshape(DEPTH, bs, WINDOW, SWA_KV)
    cache_v = cache_swa_v.reshape(DEPTH, bs, WINDOW, SWA_KV)

    bgf, gng = b_gf[:, None, :], gla_norm_g.reshape(DEPTH, 1, GLA_V)
    bm, g1, b1 = b_merge[:, None, :], ln1_g[:, None, :], ln1_b[:, None, :]
    g2, b2 = ln2_g[:, None, :], ln2_b[:, None, :]

    x_p, x_s = x_prompt.reshape(n_p, D_MODEL), x_sample.reshape(n_s, D_MODEL)
    st_s = jnp.zeros((DEPTH, bs, H_A, DK_A, DV_A), F32)
    kk_s = jnp.zeros((DEPTH, bs, WINDOW, SWA_KV), F32)
    vk_s = jnp.zeros((DEPTH, bs, WINDOW, SWA_KV), F32)
    rows_buf = _moe_rows_buffer(n_p + n_s)
    outs = [[] for _ in range(3)]
    for l in range(DEPTH):
        y_p, pk_p, route_p, rt_p, counts_p, st_p, kk_p, vk_p = _mixer_prompt(
            sinks, x_p, w_proj, wgf, bgf, gng, mats_p, tabs_p, wa, wb, wo, bm, g1, b1, wr, br, tri,
            l, bp, tp)
        qk, v, r, sq, kv, gab, glr = _inproj(x_s, w_proj, l)
        oa_s, st_s = _gla_sample(qk, v, glr, wgf, bgf, gng, mats_s, state_gla, st_s, l, bs, ts)
        ob_s, kk_s, vk_s = _swa_sample(sinks, sq, kv, tabs_s, cache_k, cache_v, kk_s, vk_s,
                                       l, bs, ts)
        y_s, pk_s, route_s, rt_s, counts = _merge(oa_s, r, ob_s, gab, x_s, wa, wb, wo, bm, g1, b1,
                                                  wr, br, tri, counts_p, l)
        route = jnp.concatenate([route_p, route_s], axis=0)
        route_t = jnp.concatenate([rt_p, rt_s], axis=1)
        x_p, x_s, rows_buf = _moe_layer(y_p, y_s, pk_p, pk_s, route, route_t, counts, rows_buf,
                                        wg, wu, wd, g2, b2, l)
        for lst, val in zip(outs, (st_p, kk_p, vk_p)):
            lst.append(val)

    st_p, kk_p, vk_p = [jnp.stack(o) for o in outs]
    return (x_p.reshape(bp, tp, D_MODEL), x_s.reshape(bs, ts, D_MODEL),
            st_p.astype(state_gla.dtype),
            kk_p.reshape(DEPTH, bp, WINDOW, KV_B, HD_B), vk_p.reshape(DEPTH, bp, WINDOW, KV_B, HD_B),
            st_s.astype(state_gla.dtype),
            kk_s.reshape(DEPTH, bs, WINDOW, KV_B, HD_B), vk_s.reshape(DEPTH, bs, WINDOW, KV_B, HD_B))
```
